```python
import math, functools
import jax, jax.numpy as jnp
from jax import lax
import numpy as np

D_MODEL = 2048
BATCH = 8
SEQ = 2048
DEPTH = 1
DEC_BATCH = 32
DEC_SEQ = 8
PAST_LEN = 8192
PAGE_SIZE = 128

N_META = 16
DK_ATT = 64
DV_ATT = 2 * DK_ATT
H_ATT = (D_MODEL // 2) // DV_ATT
ATT_W = H_ATT * DV_ATT
QK_ATT_W = H_ATT * 2 * DK_ATT
ATT_SCALE = DK_ATT ** -0.5
DK_HG = 128
DV_HG = 128
H_HG = (D_MODEL // 2) // DV_HG
HG_W = H_HG * DV_HG
QK_HG_W = H_HG * DK_HG
HG_CHUNK = 64
MIX_W = ATT_W + HG_W
IN_W = 2 * QK_ATT_W + ATT_W + 2 * QK_HG_W + 2 * HG_W
N_EXPERTS = 32
TOP_K = 4
D_FF = D_MODEL
SWIGLU_LIMIT = 7.0
SWIGLU_ALPHA = 1.702

Q_BLOCK = 128
NORM_EPS = 1e-5
NEG_INF = -1e30

kernel_name = "hymba_diffattn_hgrn2_moe_step"


def rmsnorm(x, g):
    xf = x.astype(jnp.float32)
    y = xf * lax.rsqrt(jnp.mean(xf * xf, axis=-1, keepdims=True) + NORM_EPS)
    return (y * g.astype(jnp.float32)).astype(x.dtype)


def diff_weights(s, lam):
    p = jax.nn.softmax(s, axis=-1)
    return p[:, :, 0] - lam * p[:, :, 1]


def diff_attn_prompt(q, k, v, lam):
    Bn, L = q.shape[:2]
    n_blk = -(-L // Q_BLOCK)
    qp = jnp.pad(q, ((0, 0), (0, n_blk * Q_BLOCK - L), (0, 0), (0, 0), (0, 0)))
    qb = jnp.moveaxis(qp.reshape(Bn, n_blk, Q_BLOCK, H_ATT, 2, DK_ATT), 1, 0)
    k_pos = jnp.arange(L)

    def block(args):
        i, qi = args
        q_pos = i * Q_BLOCK + jnp.arange(Q_BLOCK)
        s = jnp.einsum('bqhcd,bkhcd->bhcqk', qi, k, preferred_element_type=jnp.float32) * ATT_SCALE
        s = jnp.where(k_pos[None, :] <= q_pos[:, None], s, NEG_INF)
        w = diff_weights(s, lam)
        return jnp.einsum('bhqk,bkhd->bqhd', w.astype(v.dtype), v)

    o = lax.map(block, (jnp.arange(n_blk), qb))
    return jnp.moveaxis(o, 0, 1).reshape(Bn, n_blk * Q_BLOCK, H_ATT, DV_ATT)[:, :L]


def diff_attn_sample(q, k, v, lam, k_past, v_past):
    T = q.shape[1]
    P = k_past.shape[1]
    s_past = jnp.einsum('bqhcd,bkhcd->bhcqk', q, k_past, preferred_element_type=jnp.float32) * ATT_SCALE
    s_new = jnp.einsum('bqhcd,bkhcd->bhcqk', q, k, preferred_element_type=jnp.float32) * ATT_SCALE
    s_new = jnp.where(jnp.tril(jnp.ones((T, T), dtype=bool)), s_new, NEG_INF)
    w = diff_weights(jnp.concatenate([s_past, s_new], axis=-1), lam).astype(v.dtype)
    return (jnp.einsum('bhqk,bkhd->bqhd', w[..., :P], v_past)
            + jnp.einsum('bhqk,bkhd->bqhd', w[..., P:], v))


def gla_chunk(S, q, k, v, lf):
    c = q.shape[1]
    b = jnp.cumsum(lf, axis=1)
    causal = jnp.tril(jnp.ones((c, c), dtype=bool))
    diff = b[:, :, None] - b[:, None, :]
    decay = jnp.exp(jnp.where(causal[None, :, :, None, None], diff, -jnp.inf))
    a = jnp.einsum('bthd,bshd,btshd->bhts', q, k, decay)
    o = (jnp.einsum('bthd,bhde->bthe', q * jnp.exp(b), S)
         + jnp.einsum('bhts,bshe->bthe', a, v))
    b_last = b[:, -1]
    S = (jnp.exp(b_last)[..., None] * S
         + jnp.einsum('bshd,bshe->bhde', k * jnp.exp(b_last[:, None] - b), v))
    return S, o


def hgrn2_sweep(S, q, k, v, lf, n_lead):
    outs = []
    if n_lead > 0:
        S, o = gla_chunk(S, q[:, :n_lead], k[:, :n_lead], v[:, :n_lead], lf[:, :n_lead])
        outs.append(o)
        q, k, v, lf = (a[:, n_lead:] for a in (q, k, v, lf))
    Bn, T = q.shape[:2]
    c = HG_CHUNK if T % HG_CHUNK == 0 else T
    n = T // c

    def to_chunks(a):
        return jnp.swapaxes(a.reshape(Bn, n, c, *a.shape[2:]), 0, 1)

    def step(S_c, xs):
        return gla_chunk(S_c, *xs)

    S, o = lax.scan(step, S, (to_chunks(q), to_chunks(k), to_chunks(v), to_chunks(lf)))
    outs.append(jnp.swapaxes(o, 0, 1).reshape(Bn, T, H_HG, DV_HG))
    o_all = jnp.concatenate(outs, axis=1) if len(outs) > 1 else outs[0]
    return o_all, S


def mixer(xn, l, attend, S0, n_lead, w_in, lambda_q1, lambda_k1, lambda_q2, lambda_k2,
          attn_sub_g, hgrn_gamma, hgrn_norm_g, w_out):
    Bn, T, _ = xn.shape
    sizes = [QK_ATT_W, QK_ATT_W, ATT_W, QK_HG_W, QK_HG_W, HG_W, HG_W]
    proj = xn @ w_in[l]
    qa, ka, va, qh, fh, ih, gh = jnp.split(proj, np.cumsum(sizes)[:-1].tolist(), axis=-1)
    qa = qa.reshape(Bn, T, H_ATT, 2, DK_ATT)
    ka = ka.reshape(Bn, T, H_ATT, 2, DK_ATT)
    va = va.reshape(Bn, T, H_ATT, DV_ATT)
    lam_init = 0.8 - 0.6 * math.exp(-0.3 * l)
    f32 = jnp.float32
    lam = (jnp.exp(jnp.sum(lambda_q1[l].astype(f32) * lambda_k1[l].astype(f32)))
           - jnp.exp(jnp.sum(lambda_q2[l].astype(f32) * lambda_k2[l].astype(f32))) + lam_init)
    att = attend(qa, ka, va, lam)
    att = rmsnorm(att, attn_sub_g[l]) * (1.0 - lam_init)
    lb = jnp.cumsum(jax.nn.softmax(hgrn_gamma.astype(f32), axis=0), axis=0)[l].reshape(H_HG, DK_HG)
    f = lb + (1.0 - lb) * jax.nn.sigmoid(fh.reshape(Bn, T, H_HG, DK_HG).astype(f32))
    hg, S = hgrn2_sweep(S0, qh.reshape(Bn, T, H_HG, DK_HG).astype(f32), 1.0 - f,
                        ih.reshape(Bn, T, H_HG, DV_HG).astype(f32), jnp.log(f), n_lead)
    hg = rmsnorm(hg, hgrn_norm_g[l]).astype(xn.dtype) * jax.nn.silu(gh.reshape(Bn, T, H_HG, DV_HG))
    out = jnp.concatenate([att.reshape(Bn, T, ATT_W), hg.reshape(Bn, T, HG_W)], axis=-1) @ w_out[l]
    return out, ka.reshape(Bn, T, H_ATT, 2 * DK_ATT), va, S


def moe_ffn(x, l, w_router, b_router, w_gate_up, b_gate_up, w_down, b_down):
    shp = x.shape
    xt = x.reshape(-1, D_MODEL)
    logits = (xt @ w_router[l] + b_router[l]).astype(jnp.float32)
    top_v, top_i = lax.top_k(logits, TOP_K)
    gates = jax.nn.softmax(top_v, axis=-1)
    combine = jnp.einsum('tk,tke->te', gates,
                         jax.nn.one_hot(top_i, N_EXPERTS, dtype=jnp.float32)).astype(x.dtype)
    out = jnp.zeros_like(xt)
    for e in range(N_EXPERTS):
        gu = xt @ w_gate_up[l, e] + b_gate_up[l, e]
        gate = jnp.minimum(gu[:, :D_FF], SWIGLU_LIMIT)
        up = jnp.clip(gu[:, D_FF:], -SWIGLU_LIMIT, SWIGLU_LIMIT)
        act = (up + 1.0) * gate * jax.nn.sigmoid(SWIGLU_ALPHA * gate)
        out = out + combine[:, e:e + 1] * (act @ w_down[l, e] + b_down[l, e])
    return out.reshape(shp)


def setup_inputs(seed: int = 0) -> dict:
    key = jax.random.key(seed)
    ks = jax.random.split(key, 26)
    n_pages = PAST_LEN // PAGE_SIZE
    n_pool = (DEC_BATCH * n_pages * 5) // 4

    def nrm(k, shape, scale=1.0):
        return jax.random.normal(k, shape, jnp.float32) * scale

    page_table = jax.random.permutation(ks[5], n_pool)[:DEC_BATCH * n_pages]
    page_table = page_table.reshape(DEC_BATCH, n_pages).astype(jnp.int32)
    return {
        "x_prompt": nrm(ks[0], (BATCH, SEQ, D_MODEL)),
        "x_sample": nrm(ks[1], (DEC_BATCH, DEC_SEQ, D_MODEL)),
        "cache_k": nrm(ks[2], (DEPTH, n_pool, PAGE_SIZE, H_ATT, 2 * DK_ATT)),
        "cache_v": nrm(ks[3], (DEPTH, n_pool, PAGE_SIZE, H_ATT, DV_ATT)),
        "state_hgrn": nrm(ks[4], (DEPTH, DEC_BATCH, H_HG, DK_HG, DV_HG), 0.3),
        "page_table": page_table,
        "meta_tokens": nrm(ks[6], (N_META, D_MODEL)),
        "norm_mix_g": 1.0 + nrm(ks[7], (DEPTH, D_MODEL), 0.02),
        "w_in": nrm(ks[8], (DEPTH, D_MODEL, IN_W), D_MODEL ** -0.5),
        "lambda_q1": nrm(ks[9], (DEPTH, DK_ATT), 0.1),
        "lambda_k1": nrm(ks[10], (DEPTH, DK_ATT), 0.1),
        "lambda_q2": nrm(ks[11], (DEPTH, DK_ATT), 0.1),
        "lambda_k2": nrm(ks[12], (DEPTH, DK_ATT), 0.1),
        "attn_sub_g": 1.0 + nrm(ks[13], (DEPTH, DV_ATT), 0.02),
        "hgrn_gamma": nrm(ks[14], (DEPTH + 1, H_HG * DK_HG), 0.1),
        "hgrn_norm_g": 1.0 + nrm(ks[15], (DEPTH, DV_HG), 0.02),
        "w_out": nrm(ks[16], (DEPTH, MIX_W, D_MODEL), MIX_W ** -0.5),
        "norm_ffn_g": 1.0 + nrm(ks[17], (DEPTH, D_MODEL), 0.02),
        "w_router": nrm(ks[18], (DEPTH, D_MODEL, N_EXPERTS), D_MODEL ** -0.5),
        "b_router": nrm(ks[19], (DEPTH, N_EXPERTS), 0.01),
        "w_gate_up": nrm(ks[20], (DEPTH, N_EXPERTS, D_MODEL, 2 * D_FF), D_MODEL ** -0.5),
        "b_gate_up": nrm(ks[21], (DEPTH, N_EXPERTS, 2 * D_FF), 0.01),
        "w_down": nrm(ks[22], (DEPTH, N_EXPERTS, D_FF, D_MODEL), D_FF ** -0.5),
        "b_down": nrm(ks[23], (DEPTH, N_EXPERTS, D_MODEL), 0.01),
        "norm_final_g": 1.0 + nrm(ks[24], (D_MODEL,), 0.02),
    }


def reference(x_prompt, x_sample, cache_k, cache_v, state_hgrn, page_table, meta_tokens,
              norm_mix_g, w_in, lambda_q1, lambda_k1, lambda_q2, lambda_k2, attn_sub_g,
              hgrn_gamma, hgrn_norm_g, w_out, norm_ffn_g, w_router, b_router, w_gate_up,
              b_gate_up, w_down, b_down, norm_final_g):
    mix_w = (w_in, lambda_q1, lambda_k1, lambda_q2, lambda_k2, attn_sub_g, hgrn_gamma, hgrn_norm_g, w_out)
    ffn_w = (w_router, b_router, w_gate_up, b_gate_up, w_down, b_down)

    Bp = x_prompt.shape[0]
    meta = jnp.broadcast_to(meta_tokens.astype(x_prompt.dtype)[None], (Bp, N_META, D_MODEL))
    h = jnp.concatenate([meta, x_prompt], axis=1)
    kp, vp, sp = [], [], []
    for l in range(DEPTH):
        S0 = jnp.zeros((Bp, H_HG, DK_HG, DV_HG), jnp.float32)
        m, k_rows, v_rows, S = mixer(rmsnorm(h, norm_mix_g[l]), l, diff_attn_prompt, S0, N_META, *mix_w)
        h = h + m
        if l == DEPTH - 1:
            h = h[:, N_META:]
        h = h + moe_ffn(rmsnorm(h, norm_ffn_g[l]), l, *ffn_w)
        kp.append(k_rows)
        vp.append(v_rows)
        sp.append(S.astype(state_hgrn.dtype))
    y_prompt = rmsnorm(h, norm_final_g)

    Bd = x_sample.shape[0]
    past = page_table.shape[1] * cache_k.shape[2]
    h = x_sample
    ks_, vs_, ss_ = [], [], []
    for l in range(DEPTH):
        k_past = cache_k[l][page_table].reshape(Bd, past, H_ATT, 2, DK_ATT)
        v_past = cache_v[l][page_table].reshape(Bd, past, H_ATT, DV_ATT)
        attend = functools.partial(diff_attn_sample, k_past=k_past, v_past=v_past)
        S0 = state_hgrn[l].astype(jnp.float32)
        m, k_rows, v_rows, S = mixer(rmsnorm(h, norm_mix_g[l]), l, attend, S0, 0, *mix_w)
        h = h + m
        h = h + moe_ffn(rmsnorm(h, norm_ffn_g[l]), l, *ffn_w)
        ks_.append(k_rows)
        vs_.append(v_rows)
        ss_.append(S.astype(state_hgrn.dtype))
    y_sample = rmsnorm(h, norm_final_g)

    k_prompt = jnp.stack(kp)
    v_prompt = jnp.stack(vp)
    s_prompt = jnp.stack(sp)
    k_sample = jnp.stack(ks_)
    v_sample = jnp.stack(vs_)
    s_sample = jnp.stack(ss_)
    return (y_prompt, y_sample, k_prompt, v_prompt, s_prompt, k_sample, v_sample, s_sample)
```

```python
import functools
import math

import jax
import jax.numpy as jnp
from jax import lax
from jax.experimental import pallas as pl
from jax.experimental.pallas import tpu as pltpu

F32 = jnp.float32
BF16 = jnp.bfloat16

D_MODEL = 2048
N_META = 16
H = 8
HD = 128
DK_ATT = 64
SEG = H * HD
N_SEG = 7
ATT_SCALE = DK_ATT ** -0.5
HG_CHUNK = 64
HG_SUB = 16
N_EXPERTS = 32
TOP_K = 4
D_FF = D_MODEL
SWIGLU_LIMIT = 7.0
SWIGLU_ALPHA = 1.702
NORM_EPS = 1e-5
NEG_INF = -1e30
PAGE = 128

VMEM_LIMIT = 56 * 1024 * 1024


def _cp(sem, vmem=VMEM_LIMIT):
    return pltpu.CompilerParams(dimension_semantics=sem, vmem_limit_bytes=vmem)


def _dot(a, b):
    return jnp.dot(a, b, preferred_element_type=F32)


def _dot_nt(a, b):
    return lax.dot_general(a, b, (((1,), (1,)), ((), ())), preferred_element_type=F32)


def _dot_tn(a, b):
    return lax.dot_general(a, b, (((0,), (0,)), ((), ())), preferred_element_type=F32)


def _rms(x, g):
    return x * lax.rsqrt(jnp.mean(x * x, axis=-1, keepdims=True) + NORM_EPS) * g


def _norm_matmul_kernel(x_ref, g_ref, w_ref, o_ref, xn_ref):
    @pl.when(pl.program_id(1) == 0)
    def _():
        xn_ref[...] = _rms(x_ref[...], g_ref[...]).astype(BF16)

    o_ref[...] = _dot(xn_ref[...], w_ref[...])


def _norm_matmul(x, g, w_bf16, tm, tn):
    n, d = x.shape
    m = w_bf16.shape[1]
    return pl.pallas_call(
        _norm_matmul_kernel,
        grid=(n // tm, m // tn),
        in_specs=[
            pl.BlockSpec((tm, d), lambda i, j: (i, 0)),
            pl.BlockSpec((1, d), lambda i, j: (0, 0)),
            pl.BlockSpec((d, tn), lambda i, j: (0, j)),
        ],
        out_specs=pl.BlockSpec((tm, tn), lambda i, j: (i, j)),
        out_shape=jax.ShapeDtypeStruct((n, m), F32),
        scratch_shapes=[pltpu.VMEM((tm, d), BF16)],
        compiler_params=_cp(("parallel", "arbitrary")),
        name="norm_inproj",
    )(x, g.reshape(1, d), w_bf16)


def _split_q(q):
    lane = lax.broadcasted_iota(jnp.int32, q.shape, 1)
    q = q * ATT_SCALE
    return jnp.concatenate([jnp.where(lane < DK_ATT, q, 0.0), jnp.where(lane >= DK_ATT, q, 0.0)], axis=0).astype(BF16)


def _diff_out(acc, l, lam, g, scale):
    t = acc.shape[0] // 2
    o = acc / l
    att = o[:t] - lam * o[t:]
    return _rms(att, g) * scale


def _attn_prompt_kernel(lam_ref, q_ref, k_ref, v_ref, km_ref, vm_ref, g_ref, o_ref, kb_ref, vb_ref, kmb_ref, vmb_ref,
                        *, tq, scale):
    qi = pl.program_id(2)

    @pl.when(qi == 0)
    def _():
        kb_ref[...] = k_ref[0].astype(BF16)
        vb_ref[...] = v_ref[0].astype(BF16)
        kmb_ref[...] = km_ref[...].astype(BF16)
        vmb_ref[...] = vm_ref[...].astype(BF16)

    q2 = _split_q(q_ref[0])
    s = _dot_nt(q2, kmb_ref[...])
    col = lax.broadcasted_iota(jnp.int32, s.shape, 1)
    s = jnp.where(col < N_META, s, NEG_INF)
    m = jnp.max(s, axis=-1, keepdims=True)
    p = jnp.exp(s - m)
    l = jnp.sum(p, axis=-1, keepdims=True)
    acc = _dot(p.astype(BF16), vmb_ref[...])

    def step(s, m, l, acc, vblk):
        m_new = jnp.maximum(m, jnp.max(s, axis=-1, keepdims=True))
        alpha = jnp.exp(m - m_new)
        p = jnp.exp(s - m_new)
        l = alpha * l + jnp.sum(p, axis=-1, keepdims=True)
        acc = alpha * acc + _dot(p.astype(BF16), vblk)
        return m_new, l, acc

    def body(kj, carry):
        m, l, acc = carry
        off = pl.multiple_of(kj * tq, tq)
        s = _dot_nt(q2, kb_ref[pl.ds(off, tq), :])
        return step(s, m, l, acc, vb_ref[pl.ds(off, tq), :])

    m, l, acc = lax.fori_loop(0, qi, body, (m, l, acc))

    off = pl.multiple_of(qi * tq, tq)
    s = _dot_nt(q2, kb_ref[pl.ds(off, tq), :])
    row = lax.broadcasted_iota(jnp.int32, (tq, tq), 0)
    colq = lax.broadcasted_iota(jnp.int32, (tq, tq), 1)
    causal = colq <= row
    s = jnp.where(jnp.concatenate([causal, causal], axis=0), s, NEG_INF)
    m, l, acc = step(s, m, l, acc, vb_ref[pl.ds(off, tq), :])

    o_ref[0] = _diff_out(acc, l, lam_ref[0], g_ref[...], scale).astype(o_ref.dtype)


def _attn_prompt(lam, proj3, projm_pad, g, scale, tq=256):
    b, t, _ = proj3.shape
    kern = functools.partial(_attn_prompt_kernel, tq=tq, scale=scale)
    return pl.pallas_call(
        kern,
        grid=(b, H, t // tq),
        in_specs=[
            pl.BlockSpec(memory_space=pltpu.SMEM),
            pl.BlockSpec((1, tq, HD), lambda bi, h, qi: (bi, qi, h)),
            pl.BlockSpec((1, t, HD), lambda bi, h, qi: (bi, 0, H + h)),
            pl.BlockSpec((1, t, HD), lambda bi, h, qi: (bi, 0, 2 * H + h)),
            pl.BlockSpec((128, HD), lambda bi, h, qi: (0, H + h)),
            pl.BlockSpec((128, HD), lambda bi, h, qi: (0, 2 * H + h)),
            pl.BlockSpec((1, HD), lambda bi, h, qi: (0, 0)),
        ],
        out_specs=pl.BlockSpec((1, tq, HD), lambda bi, h, qi: (bi, qi, h)),
        out_shape=jax.ShapeDtypeStruct((b, t, SEG), BF16),
        scratch_shapes=[pltpu.VMEM((t, HD), BF16), pltpu.VMEM((t, HD), BF16),
                        pltpu.VMEM((128, HD), BF16), pltpu.VMEM((128, HD), BF16)],
        compiler_params=_cp(("parallel", "parallel", "arbitrary")),
        name="attn_prompt",
    )(lam, proj3, proj3, proj3, projm_pad, projm_pad, g.reshape(1, HD))


def _attn_decode_kernel(pt_ref, lam_ref, q_ref, kn_ref, vn_ref, kp_ref, vp_ref, g_ref, o_ref,
                        m_ref, l_ref, acc_ref, bias_ref, *, n_tok, scale):
    del pt_ref
    p_idx = pl.program_id(1)
    n_pages = pl.num_programs(1)
    rows = 2 * n_tok * H

    @pl.when((pl.program_id(0) == 0) & (p_idx == 0))
    def _():
        r = lax.broadcasted_iota(jnp.int32, bias_ref.shape, 0)
        c = lax.broadcasted_iota(jnp.int32, bias_ref.shape, 1)
        bias_ref[...] = jnp.where((r % H) == (c % H), 0.0, NEG_INF).astype(F32)

    @pl.when(p_idx == 0)
    def _():
        m_ref[...] = jnp.full(m_ref.shape, NEG_INF, F32)
        l_ref[...] = jnp.zeros(l_ref.shape, F32)
        acc_ref[...] = jnp.zeros(acc_ref.shape, F32)

    q2 = _split_q(q_ref[0])

    def step(s, vblk):
        m = m_ref[...]
        m_new = jnp.maximum(m, jnp.max(s, axis=-1, keepdims=True))
        alpha = jnp.exp(m - m_new)
        p = jnp.exp(s - m_new)
        l_ref[...] = alpha * l_ref[...] + jnp.sum(p, axis=-1, keepdims=True)
        acc_ref[...] = alpha * acc_ref[...] + _dot(p.astype(BF16), vblk)
        m_ref[...] = m_new

    kp = kp_ref[...].reshape(PAGE * H, HD).astype(BF16)
    vp = vp_ref[...].reshape(PAGE * H, HD).astype(BF16)
    step(_dot_nt(q2, kp) + bias_ref[...], vp)

    @pl.when(p_idx == n_pages - 1)
    def _():
        nk = n_tok * H
        s = _dot_nt(q2, kn_ref[0].astype(BF16)) + bias_ref[:, :nk]
        r = lax.broadcasted_iota(jnp.int32, (rows, nk), 0)
        c = lax.broadcasted_iota(jnp.int32, (rows, nk), 1)
        s = jnp.where((c // H) <= ((r // H) % n_tok), s, NEG_INF)
        step(s, vn_ref[0].astype(BF16))
        o_ref[0] = _diff_out(acc_ref[...], l_ref[...], lam_ref[0], g_ref[...], scale).astype(o_ref.dtype)


def _attn_decode(lam, page_table, q, kn, vn, cache_k, cache_v, g, scale):
    b, th, _ = q.shape
    n_tok = th // H
    n_pages = page_table.shape[1]
    rows = 2 * th
    kern = functools.partial(_attn_decode_kernel, n_tok=n_tok, scale=scale)
    grid_spec = pltpu.PrefetchScalarGridSpec(
        num_scalar_prefetch=1,
        grid=(b, n_pages),
        in_specs=[
            pl.BlockSpec(memory_space=pltpu.SMEM),
            pl.BlockSpec((1, th, HD), lambda bi, p, pt: (bi, 0, 0)),
            pl.BlockSpec((1, th, HD), lambda bi, p, pt: (bi, 0, 0)),
            pl.BlockSpec((1, th, HD), lambda bi, p, pt: (bi, 0, 0)),
            pl.BlockSpec((None, PAGE, H, HD), lambda bi, p, pt: (pt[bi * n_pages + p], 0, 0, 0)),
            pl.BlockSpec((None, PAGE, H, HD), lambda bi, p, pt: (pt[bi * n_pages + p], 0, 0, 0)),
            pl.BlockSpec((1, HD), lambda bi, p, pt: (0, 0)),
        ],
        out_specs=pl.BlockSpec((1, th, HD), lambda bi, p, pt: (bi, 0, 0)),
        scratch_shapes=[pltpu.VMEM((rows, 1), F32), pltpu.VMEM((rows, 1), F32), pltpu.VMEM((rows, HD), F32),
                        pltpu.VMEM((rows, PAGE * H), F32)],
    )
    return pl.pallas_call(
        kern,
        grid_spec=grid_spec,
        out_shape=jax.ShapeDtypeStruct((b, th, HD), BF16),
        compiler_params=_cp(("arbitrary", "arbitrary")),
        name="attn_decode",
    )(page_table.reshape(-1), lam, q, kn, vn, cache_k, cache_v, g.reshape(1, HD))


def _cumsum_rows(x):
    n = x.shape[0]
    row = lax.broadcasted_iota(jnp.int32, x.shape, 0)
    sh = 1
    while sh < n:
        x = x + jnp.where(row >= sh, pltpu.roll(x, sh, axis=0), 0.0)
        sh *= 2
    return x


def _hgrn_chunk(q, fh, iv, lb, st, sub):
    c = q.shape[0]
    f = lb + (1.0 - lb) * jax.nn.sigmoid(fh)
    kk = 1.0 - f
    b = _cumsum_rows(jnp.log(f))
    bl = b[c - 1:c]
    o_inter = _dot_nt((q * jnp.exp(b)).astype(BF16), st.astype(BF16))
    rowid = lax.broadcasted_iota(jnp.int32, (sub, 1), 0)
    outs = []
    for r0 in range(0, c, sub):
        bi, qi, ki, vi = b[r0:r0 + sub], q[r0:r0 + sub], kk[r0:r0 + sub], iv[r0:r0 + sub]
        od = jnp.zeros((sub, HD), F32)
        for s in range(sub):
            e = jnp.exp(jnp.minimum(bi - bi[s:s + 1], 0.0))
            a = jnp.sum(qi * ki[s:s + 1] * e, axis=-1, keepdims=True)
            od = od + jnp.where(rowid >= s, a, 0.0) * vi[s:s + 1]
        if r0 > 0:
            ref = b[r0 - 1:r0]
            qs = (qi * jnp.exp(bi - ref)).astype(BF16)
            ks = (kk[:r0] * jnp.exp(ref - b[:r0])).astype(BF16)
            od = od + _dot(_dot_nt(qs, ks).astype(BF16), iv[:r0].astype(BF16))
        outs.append(od)
    o = o_inter + (jnp.concatenate(outs, axis=0) if len(outs) > 1 else outs[0])
    khat = (kk * jnp.exp(bl - b)).astype(BF16)
    st_new = st * jnp.exp(bl) + _dot_tn(iv.astype(BF16), khat)
    return o, st_new


def _hgrn_kernel(q_ref, f_ref, i_ref, gt_ref, s0_ref, lb_ref, g_ref, o_ref, s_ref, st_ref, *, chunk, sub):
    ti = pl.program_id(2)

    @pl.when(ti == 0)
    def _():
        st_ref[...] = s0_ref[...].T

    n_chunks = q_ref.shape[1] // chunk
    lb = lb_ref[0]
    g = g_ref[...]

    def body(ci, carry):
        r = pl.multiple_of(ci * chunk, chunk)
        sl = pl.ds(r, chunk)
        o, st_new = _hgrn_chunk(q_ref[0, sl, :], f_ref[0, sl, :], i_ref[0, sl, :], lb, st_ref[...], sub)
        st_ref[...] = st_new
        gt = gt_ref[0, sl, :]
        o_ref[0, sl, :] = (_rms(o, g) * (gt * jax.nn.sigmoid(gt))).astype(o_ref.dtype)
        return carry

    lax.fori_loop(0, n_chunks, body, 0)

    @pl.when(ti == pl.num_programs(2) - 1)
    def _():
        s_ref[...] = st_ref[...].T


def _hgrn(proj3, s0, lb, g, chunk, tb):
    b, t, _ = proj3.shape
    sub = min(chunk, HG_SUB)
    s0_b = (lambda bi: bi) if s0.shape[0] == b else (lambda bi: 0)
    kern = functools.partial(_hgrn_kernel, chunk=chunk, sub=sub)
    seg = lambda k: pl.BlockSpec((1, tb, HD), lambda bi, h, ti: (bi, ti, k * H + h))
    return pl.pallas_call(
        kern,
        grid=(b, H, t // tb),
        in_specs=[
            seg(3), seg(4), seg(5), seg(6),
            pl.BlockSpec((None, None, HD, HD), lambda bi, h, ti: (s0_b(bi), h, 0, 0)),
            pl.BlockSpec((1, 1, HD), lambda bi, h, ti: (h, 0, 0)),
            pl.BlockSpec((1, HD), lambda bi, h, ti: (0, 0)),
        ],
        out_specs=[
            pl.BlockSpec((1, tb, HD), lambda bi, h, ti: (bi, ti, h)),
            pl.BlockSpec((None, None, HD, HD), lambda bi, h, ti: (bi, h, 0, 0)),
        ],
        out_shape=[jax.ShapeDtypeStruct((b, t, SEG), BF16), jax.ShapeDtypeStruct((b, H, HD, HD), F32)],
        scratch_shapes=[pltpu.VMEM((HD, HD), F32)],
        compiler_params=_cp(("parallel", "parallel", "arbitrary")),
        name="hgrn",
    )(proj3, proj3, proj3, proj3, s0, lb, g.reshape(1, HD))


def _outproj_router_kernel(att_ref, hg_ref, x_ref, wo_ref, g_ref, wr_ref, br_ref, cin_ref,
                           h_ref, xn_ref, ti_ref, gate_ref, rank_ref, cout_ref, tri_ref, cnt_ref, *, tm):
    i = pl.program_id(0)

    @pl.when(i == 0)
    def _():
        r = lax.broadcasted_iota(jnp.int32, (tm, tm), 0)
        c = lax.broadcasted_iota(jnp.int32, (tm, tm), 1)
        tri_ref[...] = jnp.where(c < r, 1.0, 0.0).astype(BF16)
        cnt_ref[...] = cin_ref[...]

    half = att_ref.shape[1]
    mix = _dot(att_ref[...], wo_ref[:half, :]) + _dot(hg_ref[...], wo_ref[half:, :])
    h = x_ref[...] + mix
    h_ref[...] = h
    xn = _rms(h, g_ref[...])
    xn_ref[...] = xn
    logits = jnp.dot(xn, wr_ref[...], preferred_element_type=F32, precision=lax.Precision.HIGHEST) + br_ref[...]

    lane = lax.broadcasted_iota(jnp.int32, logits.shape, 1).astype(F32)
    k_lane = lax.broadcasted_iota(jnp.int32, (tm, TOP_K), 1)
    work = logits
    sel = jnp.zeros(logits.shape, F32)
    top_v, top_i = [], []
    for _ in range(TOP_K):
        mx = jnp.max(work, axis=-1, keepdims=True)
        idx = jnp.min(jnp.where(work == mx, lane, float(N_EXPERTS)), axis=-1, keepdims=True)
        hit = lane == idx
        sel = jnp.where(hit, 1.0, sel)
        work = jnp.where(hit, -jnp.inf, work)
        top_v.append(mx)
        top_i.append(idx)
    ex = [jnp.exp(v - top_v[0]) for v in top_v]
    den = ex[0] + ex[1] + ex[2] + ex[3]
    before = _dot(tri_ref[...], sel.astype(BF16)) + cnt_ref[...]
    ti_out = jnp.zeros((tm, TOP_K), jnp.int32)
    gate_out = jnp.zeros((tm, TOP_K), F32)
    rank_out = jnp.zeros((tm, TOP_K), jnp.int32)
    for k in range(TOP_K):
        rk = jnp.sum(jnp.where(lane == top_i[k], before, 0.0), axis=-1, keepdims=True).astype(jnp.int32)
        ti_out = jnp.where(k_lane == k, top_i[k].astype(jnp.int32), ti_out)
        gate_out = jnp.where(k_lane == k, ex[k] / den, gate_out)
        rank_out = jnp.where(k_lane == k, rk, rank_out)
    ti_ref[...] = ti_out
    gate_ref[...] = gate_out
    rank_ref[...] = rank_out
    cnt_ref[...] = cnt_ref[...] + jnp.sum(sel, axis=0, keepdims=True)
    cout_ref[...] = cnt_ref[...]


def _outproj_router(att, hg, x, wo_bf16, g, wr, br, cnt_in, tm):
    n, d = x.shape
    half = att.shape[1]
    kern = functools.partial(_outproj_router_kernel, tm=tm)
    row = lambda w: pl.BlockSpec((tm, w), lambda i: (i, 0))
    full = lambda a, b: pl.BlockSpec((a, b), lambda i: (0, 0))
    return pl.pallas_call(
        kern,
        grid=(n // tm,),
        in_specs=[row(half), row(half), row(d), full(2 * half, d), full(1, d), full(d, N_EXPERTS),
                  full(1, N_EXPERTS), full(1, N_EXPERTS)],
        out_specs=[row(d), row(d), row(TOP_K), row(TOP_K), row(TOP_K), full(1, N_EXPERTS)],
        out_shape=[jax.ShapeDtypeStruct((n, d), F32), jax.ShapeDtypeStruct((n, d), F32),
                   jax.ShapeDtypeStruct((n, TOP_K), jnp.int32), jax.ShapeDtypeStruct((n, TOP_K), F32),
                   jax.ShapeDtypeStruct((n, TOP_K), jnp.int32), jax.ShapeDtypeStruct((1, N_EXPERTS), F32)],
        scratch_shapes=[pltpu.VMEM((tm, tm), BF16), pltpu.VMEM((1, N_EXPERTS), F32)],
        compiler_params=_cp(("arbitrary",)),
        name="outproj_router",
    )(att, hg, x, wo_bf16, g.reshape(1, d), wr, br.reshape(1, N_EXPERTS), cnt_in)


def _row_copy(src_ref, dst_ref, s, d, sem):
    return pltpu.make_async_copy(src_ref.at[pl.ds(s, 1), :], dst_ref.at[pl.ds(d, 1), :], sem)


def _dispatch_kernel(pos_ref, x_ref, xs_in_ref, xs_ref, sem, *, tm):
    del xs_in_ref

    def start(r, c):
        for k in range(TOP_K):
            _row_copy(x_ref, xs_ref, r, pos_ref[r * TOP_K + k], sem).start()
        return c

    lax.fori_loop(0, tm, start, 0)

    def wait(r, c):
        for k in range(TOP_K):
            _row_copy(x_ref, xs_ref, r, pos_ref[r * TOP_K + k], sem).wait()
        return c

    lax.fori_loop(0, tm, wait, 0)


def _dispatch(pos, xn, xs, tm):
    n, d = xn.shape
    kern = functools.partial(_dispatch_kernel, tm=tm)
    return pl.pallas_call(
        kern,
        grid=(n // tm,),
        in_specs=[
            pl.BlockSpec((tm * TOP_K,), lambda i: (i,), memory_space=pltpu.SMEM),
            pl.BlockSpec((tm, d), lambda i: (i, 0)),
            pl.BlockSpec(memory_space=pl.ANY),
        ],
        out_specs=pl.BlockSpec(memory_space=pl.ANY),
        out_shape=jax.ShapeDtypeStruct(xs.shape, xs.dtype),
        scratch_shapes=[pltpu.SemaphoreType.DMA],
        input_output_aliases={2: 0},
        compiler_params=_cp(("arbitrary",)),
        name="dispatch",
    )(pos, xn, xs)


def _expert_kernel(te_ref, nv_ref, x_ref, wg_ref, wu_ref, bg_ref, bu_ref, wd_ref, bd_ref, y_ref):
    del te_ref
    i, f = pl.program_id(0), pl.program_id(1)
    valid = i < nv_ref[0]

    @pl.when(f == 0)
    def _():
        y_ref[...] = jnp.broadcast_to(bd_ref[0], y_ref.shape)

    @pl.when(valid)
    def _():
        x = x_ref[...].astype(BF16)
        gate = jnp.minimum(_dot(x, wg_ref[...].astype(BF16)) + bg_ref[0], SWIGLU_LIMIT)
        up = jnp.clip(_dot(x, wu_ref[...].astype(BF16)) + bu_ref[0], -SWIGLU_LIMIT, SWIGLU_LIMIT)
        act = (up + 1.0) * gate * jax.nn.sigmoid(SWIGLU_ALPHA * gate)
        y_ref[...] += _dot(act.astype(BF16), wd_ref[...].astype(BF16))


def _experts(tile_expert, n_valid, xs, w_gate_up, b_gate_up, w_down, b_down, tm, tf):
    slots, d = xs.shape
    dff = w_down.shape[1]
    nf = dff // tf
    grid_spec = pltpu.PrefetchScalarGridSpec(
        num_scalar_prefetch=2,
        grid=(slots // tm, nf),
        in_specs=[
            pl.BlockSpec((tm, d), lambda i, f, te, nv: (i, 0)),
            pl.BlockSpec((None, d, tf), lambda i, f, te, nv: (te[i], 0, f)),
            pl.BlockSpec((None, d, tf), lambda i, f, te, nv: (te[i], 0, nf + f)),
            pl.BlockSpec((None, 1, tf), lambda i, f, te, nv: (te[i], 0, f)),
            pl.BlockSpec((None, 1, tf), lambda i, f, te, nv: (te[i], 0, nf + f)),
            pl.BlockSpec((None, tf, d), lambda i, f, te, nv: (te[i], f, 0)),
            pl.BlockSpec((None, 1, d), lambda i, f, te, nv: (te[i], 0, 0)),
        ],
        out_specs=pl.BlockSpec((tm, d), lambda i, f, te, nv: (i, 0)),
    )
    return pl.pallas_call(
        _expert_kernel,
        grid_spec=grid_spec,
        out_shape=jax.ShapeDtypeStruct((slots, d), F32),
        compiler_params=_cp(("arbitrary", "arbitrary")),
        name="experts",
    )(tile_expert, n_valid, xs, w_gate_up, w_gate_up, b_gate_up, b_gate_up, w_down, b_down)


def _combine_kernel(pos_ref, h_ref, gate_ref, g_ref, ys_ref, o_ref, buf_ref, sem, *, tm):
    def start(r, c):
        for k in range(TOP_K):
            _row_copy(ys_ref, buf_ref.at[k], pos_ref[r * TOP_K + k], r, sem).start()
        return c

    lax.fori_loop(0, tm, start, 0)

    def wait(r, c):
        for k in range(TOP_K):
            _row_copy(ys_ref, buf_ref.at[k], pos_ref[r * TOP_K + k], r, sem).wait()
        return c

    lax.fori_loop(0, tm, wait, 0)

    gates = gate_ref[...]
    out = h_ref[...]
    for k in range(TOP_K):
        out = out + gates[:, k:k + 1] * buf_ref[k]
    o_ref[...] = _rms(out, g_ref[...])


def _combine(pos, h, gates, g, ys, tm):
    n, d = h.shape
    kern = functools.partial(_combine_kernel, tm=tm)
    return pl.pallas_call(
        kern,
        grid=(n // tm,),
        in_specs=[
            pl.BlockSpec((tm * TOP_K,), lambda i: (i,), memory_space=pltpu.SMEM),
            pl.BlockSpec((tm, d), lambda i: (i, 0)),
            pl.BlockSpec((tm, TOP_K), lambda i: (i, 0)),
            pl.BlockSpec((1, d), lambda i: (0, 0)),
            pl.BlockSpec(memory_space=pl.ANY),
        ],
        out_specs=pl.BlockSpec((tm, d), lambda i: (i, 0)),
        out_shape=jax.ShapeDtypeStruct((n, d), F32),
        scratch_shapes=[pltpu.VMEM((TOP_K, tm, d), F32), pltpu.SemaphoreType.DMA],
        compiler_params=_cp(("arbitrary",)),
        name="combine",
    )(pos, h, gates, g.reshape(1, d), ys)


EXPERT_TM = 512
EXPERT_TF = 256
TOKEN_TM = 256


def kernel(x_prompt, x_sample, cache_k, cache_v, state_hgrn, page_table, meta_tokens, norm_mix_g, w_in, lambda_q1, lambda_k1, lambda_q2, lambda_k2, attn_sub_g, hgrn_gamma, hgrn_norm_g, w_out, norm_ffn_g, w_router, b_router, w_gate_up, b_gate_up, w_down, b_down, norm_final_g):
    l = 0
    bp, tp, d = x_prompt.shape
    bs, ts, _ = x_sample.shape
    n_p, n_s = bp * tp, bs * ts

    lam_init = 0.8 - 0.6 * math.exp(-0.3 * l)
    lam = (jnp.exp(jnp.sum(lambda_q1[l].astype(F32) * lambda_k1[l].astype(F32)))
           - jnp.exp(jnp.sum(lambda_q2[l].astype(F32) * lambda_k2[l].astype(F32))) + lam_init).reshape(1)
    sub_scale = 1.0 - lam_init
    lb = jnp.cumsum(jax.nn.softmax(hgrn_gamma.astype(F32), axis=0), axis=0)[l].reshape(H, 1, HD)
    w_in_b = w_in[l].astype(BF16)
    w_out_b = w_out[l].astype(BF16)

    proj_p = _norm_matmul(x_prompt.reshape(n_p, d), norm_mix_g[l], w_in_b, 1024, SEG)
    extra = jnp.concatenate([meta_tokens.astype(F32), x_sample.reshape(n_s, d)], axis=0)
    proj_e = _norm_matmul(extra, norm_mix_g[l], w_in_b, extra.shape[0], SEG)
    proj_m, proj_s = proj_e[:N_META], proj_e[N_META:]
    proj_p3 = proj_p.reshape(bp, tp, N_SEG * SEG)

    projm_pad = jnp.pad(proj_m, ((0, 128 - N_META), (0, 0)))
    att_p = _attn_prompt(lam, proj_p3, projm_pad, attn_sub_g[l], sub_scale)
    zero_state = jnp.zeros((1, H, HD, HD), F32)
    _, s_meta = _hgrn(proj_m.reshape(1, N_META, N_SEG * SEG), zero_state, lb, hgrn_norm_g[l], N_META, N_META)
    hg_p, s_prompt = _hgrn(proj_p3, s_meta, lb, hgrn_norm_g[l], HG_CHUNK, 512)

    rows_s = lambda k: proj_s[:, k * SEG:(k + 1) * SEG].reshape(bs, ts * H, HD)
    att_s = _attn_decode(lam, page_table, rows_s(0), rows_s(1), rows_s(2), cache_k[l], cache_v[l],
                         attn_sub_g[l], sub_scale)
    hg_s, s_sample = _hgrn(proj_s.reshape(bs, ts, N_SEG * SEG), state_hgrn[l].astype(F32), lb, hgrn_norm_g[l], ts, ts)

    cnt0 = jnp.zeros((1, N_EXPERTS), F32)
    h_p, xn_p, ti_p, gate_p, rank_p, cnt1 = _outproj_router(
        att_p.reshape(n_p, SEG), hg_p.reshape(n_p, SEG), x_prompt.reshape(n_p, d), w_out_b, norm_ffn_g[l],
        w_router[l], b_router[l], cnt0, TOKEN_TM)
    h_s, xn_s, ti_s, gate_s, rank_s, cnt2 = _outproj_router(
        att_s.reshape(n_s, SEG), hg_s.reshape(n_s, SEG), x_sample.reshape(n_s, d), w_out_b, norm_ffn_g[l],
        w_router[l], b_router[l], cnt1, TOKEN_TM)

    n_tok = n_p + n_s
    n_tiles = (n_tok * TOP_K) // EXPERT_TM + N_EXPERTS
    counts = cnt2[0].astype(jnp.int32)
    tiles_per = (counts + EXPERT_TM - 1) // EXPERT_TM
    tile_end = jnp.cumsum(tiles_per)
    offsets = (tile_end - tiles_per) * EXPERT_TM
    n_valid = tile_end[-1:]
    tile_expert = jnp.minimum(jnp.searchsorted(tile_end, jnp.arange(n_tiles, dtype=jnp.int32), side="right"),
                              N_EXPERTS - 1).astype(jnp.int32)
    last_expert = tile_expert[jnp.maximum(n_valid[0] - 1, 0)]
    tile_expert = jnp.where(jnp.arange(n_tiles) < n_valid[0], tile_expert, last_expert)
    pos_p = (offsets[ti_p] + rank_p).reshape(-1)
    pos_s = (offsets[ti_s] + rank_s).reshape(-1)

    xs = jnp.zeros((n_tiles * EXPERT_TM, d), F32)
    xs = _dispatch(pos_p, xn_p, xs, TOKEN_TM)
    xs = _dispatch(pos_s, xn_s, xs, TOKEN_TM)
    ys = _experts(tile_expert, n_valid, xs, w_gate_up[l], b_gate_up[l].reshape(N_EXPERTS, 1, 2 * D_FF),
                  w_down[l], b_down[l].reshape(N_EXPERTS, 1, d), EXPERT_TM, EXPERT_TF)
    y_p = _combine(pos_p, h_p, gate_p, norm_final_g, ys, TOKEN_TM)
    y_s = _combine(pos_s, h_s, gate_s, norm_final_g, ys, TOKEN_TM)

    def kv_prompt(k):
        own = proj_p3[:, :, k * SEG:(k + 1) * SEG]
        meta = jnp.broadcast_to(proj_m[None, :, k * SEG:(k + 1) * SEG], (bp, N_META, SEG))
        return jnp.concatenate([meta, own], axis=1).reshape(1, bp, N_META + tp, H, HD)

    k_sample = proj_s[:, SEG:2 * SEG].reshape(1, bs, ts, H, HD)
    v_sample = proj_s[:, 2 * SEG:3 * SEG].reshape(1, bs, ts, H, HD)
    return (y_p.reshape(bp, tp, d), y_s.reshape(bs, ts, d), kv_prompt(1), kv_prompt(2), s_prompt[None],
            k_sample, v_sample, s_sample[None].astype(state_hgrn.dtype))
```

```python
import functools
import math

import jax
import jax.numpy as jnp
from jax import lax
from jax.experimental import pallas as pl
from jax.experimental.pallas import tpu as pltpu

F32 = jnp.float32
BF16 = jnp.bfloat16

D_MODEL = 2048
N_META = 16
H = 8
HD = 128
DK_ATT = 64
SEG = H * HD
N_SEG = 7
ATT_SCALE = DK_ATT ** -0.5
HG_CHUNK = 64
HG_SUB = 16
HG_UNROLL = 8
N_EXPERTS = 32
TOP_K = 4
D_FF = D_MODEL
SWIGLU_LIMIT = 7.0
SWIGLU_ALPHA = 1.702
NORM_EPS = 1e-5
NEG_INF = -1e30
LOG2E = 1.4426950408889634
PAGE = 128

VMEM_LIMIT = 56 * 1024 * 1024


def _cp(sem, vmem=VMEM_LIMIT):
    return pltpu.CompilerParams(dimension_semantics=sem, vmem_limit_bytes=vmem)


def _dot(a, b):
    return jnp.dot(a, b, preferred_element_type=F32)


def _dot_nt(a, b):
    return lax.dot_general(a, b, (((1,), (1,)), ((), ())), preferred_element_type=F32)


def _dot_tn(a, b):
    return lax.dot_general(a, b, (((0,), (0,)), ((), ())), preferred_element_type=F32)


def _rms(x, g):
    return x * lax.rsqrt(jnp.mean(x * x, axis=-1, keepdims=True) + NORM_EPS) * g


def _norm_matmul_kernel(x_ref, g_ref, w_ref, o_ref, xn_ref):
    @pl.when(pl.program_id(1) == 0)
    def _():
        xn_ref[...] = _rms(x_ref[...], g_ref[...]).astype(BF16)

    o_ref[...] = _dot(xn_ref[...], w_ref[...])


def _norm_matmul(x, g, w_bf16, tm, tn):
    n, d = x.shape
    m = w_bf16.shape[1]
    return pl.pallas_call(
        _norm_matmul_kernel,
        grid=(n // tm, m // tn),
        in_specs=[
            pl.BlockSpec((tm, d), lambda i, j: (i, 0)),
            pl.BlockSpec((1, d), lambda i, j: (0, 0)),
            pl.BlockSpec((d, tn), lambda i, j: (0, j)),
        ],
        out_specs=pl.BlockSpec((tm, tn), lambda i, j: (i, j)),
        out_shape=jax.ShapeDtypeStruct((n, m), F32),
        scratch_shapes=[pltpu.VMEM((tm, d), BF16)],
        compiler_params=_cp(("parallel", "arbitrary")),
        name="norm_inproj",
    )(x, g.reshape(1, d), w_bf16)


ATT_HEADS_PER_STEP = 4


def _attn_prompt_kernel(lam_ref, q_ref, k_ref, v_ref, km_ref, vm_ref, g_ref, o_ref, kb_ref, vt_ref, kmb_ref, vmt_ref,
                        acc_ref, s_ref, *, tq, scale):
    qi = pl.program_id(2)
    heads = range(ATT_HEADS_PER_STEP)
    lanes = lambda hh: slice(hh * HD, (hh + 1) * HD)

    @pl.when(qi == 0)
    def _():
        for hh in heads:
            kb_ref[hh] = k_ref[0, :, lanes(hh)].astype(BF16)
            vt_ref[hh] = v_ref[0, :, lanes(hh)].T.astype(BF16)
            kmb_ref[hh] = km_ref[:, lanes(hh)].astype(BF16)
            vmt_ref[hh] = vm_ref[:, lanes(hh)].T.astype(BF16)

    def split_q(hh):
        qt = (q_ref[0, :, lanes(hh)] * (ATT_SCALE * LOG2E)).T
        d_idx = lax.broadcasted_iota(jnp.int32, qt.shape, 0)
        return jnp.concatenate([jnp.where(d_idx < DK_ATT, qt, 0.0), jnp.where(d_idx >= DK_ATT, qt, 0.0)],
                               axis=1).astype(BF16)

    q2t = [split_q(hh) for hh in heads]

    def update(hh, s, m, l, vt_blk):
        m_new = jnp.maximum(m, jnp.max(s, axis=0, keepdims=True))
        alpha = jnp.exp2(m - m_new)
        p = jnp.exp2(s - m_new)
        l = alpha * l + jnp.sum(p, axis=0, keepdims=True)
        acc_ref[hh] = acc_ref[hh] * alpha + _dot(vt_blk, p.astype(BF16))
        return m_new, l

    def scores(hh, kj):
        off = pl.multiple_of(kj * tq, tq)
        return _dot(kb_ref[hh, pl.ds(off, tq), :], q2t[hh])

    carry = []
    for hh in heads:
        s = _dot(kmb_ref[hh], q2t[hh])
        m = jnp.max(s, axis=0, keepdims=True)
        p = jnp.exp2(s - m)
        carry += [m, jnp.sum(p, axis=0, keepdims=True)]
        acc_ref[hh] = _dot(vmt_ref[hh], p.astype(BF16))
        s_ref[hh] = scores(hh, 0)

    def body(kj, carry):
        off = pl.multiple_of(kj * tq, tq)
        out = []
        for hh in heads:
            s_cur = s_ref[hh]
            s_ref[hh] = scores(hh, kj + 1)
            out += update(hh, s_cur, carry[2 * hh], carry[2 * hh + 1], vt_ref[hh, :, pl.ds(off, tq)])
        return tuple(out)

    carry = lax.fori_loop(0, qi, body, tuple(carry))

    off = pl.multiple_of(qi * tq, tq)
    key = lax.broadcasted_iota(jnp.int32, (tq, tq), 0)
    qry = lax.broadcasted_iota(jnp.int32, (tq, tq), 1)
    causal = key <= qry
    causal2 = jnp.concatenate([causal, causal], axis=1)
    for hh in heads:
        s = jnp.where(causal2, s_ref[hh], NEG_INF)
        _, l = update(hh, s, carry[2 * hh], carry[2 * hh + 1], vt_ref[hh, :, pl.ds(off, tq)])
        o = acc_ref[hh] / l
        att = (o[:, :tq] - lam_ref[0] * o[:, tq:]).T
        o_ref[0, :, lanes(hh)] = (_rms(att, g_ref[...]) * scale).astype(o_ref.dtype)


def _attn_prompt(lam, proj3, projm, g, scale, tq=256):
    b, t, _ = proj3.shape
    hp = ATT_HEADS_PER_STEP
    w = hp * HD
    nh = H // hp
    kern = functools.partial(_attn_prompt_kernel, tq=tq, scale=scale)
    return pl.pallas_call(
        kern,
        grid=(b, nh, t // tq),
        in_specs=[
            pl.BlockSpec(memory_space=pltpu.SMEM),
            pl.BlockSpec((1, tq, w), lambda bi, h, qi: (bi, qi, h)),
            pl.BlockSpec((1, t, w), lambda bi, h, qi: (bi, 0, nh + h)),
            pl.BlockSpec((1, t, w), lambda bi, h, qi: (bi, 0, 2 * nh + h)),
            pl.BlockSpec((N_META, w), lambda bi, h, qi: (0, nh + h)),
            pl.BlockSpec((N_META, w), lambda bi, h, qi: (0, 2 * nh + h)),
            pl.BlockSpec((1, HD), lambda bi, h, qi: (0, 0)),
        ],
        out_specs=pl.BlockSpec((1, tq, w), lambda bi, h, qi: (bi, qi, h)),
        out_shape=jax.ShapeDtypeStruct((b, t, SEG), BF16),
        scratch_shapes=[pltpu.VMEM((hp, t, HD), BF16), pltpu.VMEM((hp, HD, t), BF16),
                        pltpu.VMEM((hp, N_META, HD), BF16), pltpu.VMEM((hp, HD, N_META), BF16),
                        pltpu.VMEM((hp, HD, 2 * tq), F32), pltpu.VMEM((hp, tq, 2 * tq), F32)],
        compiler_params=_cp(("parallel", "parallel", "arbitrary")),
        name="attn_prompt",
    )(lam, proj3, proj3, proj3, projm, projm, g.reshape(1, HD))


DEC_PAGES = 8


def _split_q(q):
    lane = lax.broadcasted_iota(jnp.int32, q.shape, 1)
    q = q * (ATT_SCALE * LOG2E)
    return jnp.concatenate([jnp.where(lane < DK_ATT, q, 0.0), jnp.where(lane >= DK_ATT, q, 0.0)], axis=0).astype(BF16)


def _attn_decode_kernel(pt_ref, lam_ref, q_ref, kn_ref, vn_ref, ck_ref, cv_ref, g_ref, o_ref,
                        kbuf_ref, vbuf_ref, sem, m_ref, l_ref, acc_ref, bias_ref, *, n_tok, n_pages, scale):
    bi, gi = pl.program_id(0), pl.program_id(1)
    n_groups = pl.num_programs(1)
    step = bi * n_groups + gi
    slot = step % 2
    rows = 2 * n_tok * H

    def page_copies(st, sl):
        out = []
        for j in range(DEC_PAGES):
            page = pt_ref[st * DEC_PAGES + j]
            out.append(pltpu.make_async_copy(ck_ref.at[page], kbuf_ref.at[sl, j], sem.at[0, sl]))
            out.append(pltpu.make_async_copy(cv_ref.at[page], vbuf_ref.at[sl, j], sem.at[1, sl]))
        return out

    @pl.when(step == 0)
    def _():
        for c in page_copies(step, slot):
            c.start()
        r = lax.broadcasted_iota(jnp.int32, bias_ref.shape, 0)
        c = lax.broadcasted_iota(jnp.int32, bias_ref.shape, 1)
        bias_ref[...] = jnp.where((r % H) == (c % H), 0.0, NEG_INF).astype(F32)

    @pl.when(step + 1 < pl.num_programs(0) * n_groups)
    def _():
        for c in page_copies(step + 1, 1 - slot):
            c.start()

    @pl.when(gi == 0)
    def _():
        m_ref[...] = jnp.full(m_ref.shape, NEG_INF, F32)
        l_ref[...] = jnp.zeros(l_ref.shape, F32)
        acc_ref[...] = jnp.zeros(acc_ref.shape, F32)

    q2 = _split_q(q_ref[0])

    def step_update(s, vblk):
        m = m_ref[...]
        m_new = jnp.maximum(m, jnp.max(s, axis=-1, keepdims=True))
        alpha = jnp.exp2(m - m_new)
        p = jnp.exp2(s - m_new)
        l_ref[...] = alpha * l_ref[...] + jnp.sum(p, axis=-1, keepdims=True)
        acc_ref[...] = alpha * acc_ref[...] + _dot(p.astype(BF16), vblk)
        m_ref[...] = m_new

    for c in page_copies(step, slot):
        c.wait()
    nkeys = DEC_PAGES * PAGE * H
    kp = kbuf_ref[slot].reshape(nkeys, HD).astype(BF16)
    vp = vbuf_ref[slot].reshape(nkeys, HD).astype(BF16)
    step_update(_dot_nt(q2, kp) + bias_ref[...], vp)

    @pl.when(gi == n_groups - 1)
    def _():
        nk = n_tok * H
        s = _dot_nt(q2, kn_ref[0].astype(BF16)) + bias_ref[:, :nk]
        r = lax.broadcasted_iota(jnp.int32, (rows, nk), 0)
        c = lax.broadcasted_iota(jnp.int32, (rows, nk), 1)
        s = jnp.where((c // H) <= ((r // H) % n_tok), s, NEG_INF)
        step_update(s, vn_ref[0].astype(BF16))
        o = acc_ref[...] / l_ref[...]
        half = rows // 2
        att = o[:half] - lam_ref[0] * o[half:]
        o_ref[0] = (_rms(att, g_ref[...]) * scale).astype(o_ref.dtype)


def _attn_decode(lam, page_table, q, kn, vn, cache_k, cache_v, g, scale):
    b, th, _ = q.shape
    n_tok = th // H
    n_pages = page_table.shape[1]
    rows = 2 * th
    kern = functools.partial(_attn_decode_kernel, n_tok=n_tok, n_pages=n_pages, scale=scale)
    grid_spec = pltpu.PrefetchScalarGridSpec(
        num_scalar_prefetch=1,
        grid=(b, n_pages // DEC_PAGES),
        in_specs=[
            pl.BlockSpec(memory_space=pltpu.SMEM),
            pl.BlockSpec((1, th, HD), lambda bi, p, pt: (bi, 0, 0)),
            pl.BlockSpec((1, th, HD), lambda bi, p, pt: (bi, 0, 0)),
            pl.BlockSpec((1, th, HD), lambda bi, p, pt: (bi, 0, 0)),
            pl.BlockSpec(memory_space=pl.ANY),
            pl.BlockSpec(memory_space=pl.ANY),
            pl.BlockSpec((1, HD), lambda bi, p, pt: (0, 0)),
        ],
        out_specs=pl.BlockSpec((1, th, HD), lambda bi, p, pt: (bi, 0, 0)),
        scratch_shapes=[pltpu.VMEM((2, DEC_PAGES, PAGE, H, HD), F32), pltpu.VMEM((2, DEC_PAGES, PAGE, H, HD), F32),
                        pltpu.SemaphoreType.DMA((2, 2)),
                        pltpu.VMEM((rows, 1), F32), pltpu.VMEM((rows, 1), F32), pltpu.VMEM((rows, HD), F32),
                        pltpu.VMEM((rows, DEC_PAGES * PAGE * H), F32)],
    )
    return pl.pallas_call(
        kern,
        grid_spec=grid_spec,
        out_shape=jax.ShapeDtypeStruct((b, th, HD), BF16),
        compiler_params=_cp(("arbitrary", "arbitrary")),
        name="attn_decode",
    )(page_table.reshape(-1), lam, q, kn, vn, cache_k, cache_v, g.reshape(1, HD))


def _hgrn_intra(q, kk, iv, b, sub):
    c = q.shape[0]
    t_idx = lax.broadcasted_iota(jnp.int32, (sub, HD), 0)
    s_idx = lax.broadcasted_iota(jnp.int32, (sub, HD), 1)
    outs = []
    for r0 in range(0, c, sub):
        bi, qi, ki, vi = b[r0:r0 + sub], q[r0:r0 + sub], kk[r0:r0 + sub], iv[r0:r0 + sub]
        a = jnp.zeros((sub, HD), F32)
        for s in range(sub):
            e = jnp.exp2(bi - bi[s:s + 1])
            a = jnp.where(s_idx == s, jnp.sum(qi * ki[s:s + 1] * e, axis=-1, keepdims=True), a)
        a = jnp.where(s_idx <= t_idx, a, 0.0)
        od = _dot(a[:, :sub].astype(BF16), vi.astype(BF16))
        if r0 > 0:
            ref = b[r0 - 1:r0]
            qs = (qi * jnp.exp2(bi - ref)).astype(BF16)
            ks = (kk[:r0] * jnp.exp2(ref - b[:r0])).astype(BF16)
            od = od + _dot(_dot_nt(qs, ks).astype(BF16), iv[:r0].astype(BF16))
        outs.append(od)
    return jnp.concatenate(outs, axis=0) if len(outs) > 1 else outs[0]


def _hgrn_kernel(q_ref, f_ref, i_ref, gt_ref, s0_ref, lb_ref, g_ref, o_ref, s_ref, st_ref, b_ref, kk_ref, oi_ref,
                 *, chunk, sub):
    ti = pl.program_id(2)
    tb = q_ref.shape[1]
    n_chunks = tb // chunk

    @pl.when(ti == 0)
    def _():
        st_ref[...] = s0_ref[...].T

    lb = lb_ref[0]
    f = lb + (1.0 - lb) * jax.nn.sigmoid(f_ref[0])
    kk_ref[...] = 1.0 - f
    b = jnp.log(f) * LOG2E
    rc = lax.broadcasted_iota(jnp.int32, b.shape, 0) % chunk
    sh = 1
    while sh < chunk:
        b = b + jnp.where(rc >= sh, pltpu.roll(b, sh, axis=0), 0.0)
        sh *= 2
    b_ref[...] = b

    def intra(ci, carry):
        sl = pl.ds(pl.multiple_of(ci * chunk, chunk), chunk)
        oi_ref[sl, :] = _hgrn_intra(q_ref[0, sl, :], kk_ref[sl, :], i_ref[0, sl, :], b_ref[sl, :], sub)
        return carry

    lax.fori_loop(0, n_chunks, intra, 0, unroll=min(n_chunks, HG_UNROLL))

    g = g_ref[...]
    for ci in range(n_chunks):
        sl = slice(ci * chunk, (ci + 1) * chunk)
        bc = b_ref[sl, :]
        bl = bc[chunk - 1:chunk]
        iv = i_ref[0, sl, :].astype(BF16)
        st = st_ref[...]
        o = oi_ref[sl, :] + _dot_nt((q_ref[0, sl, :] * jnp.exp2(bc)).astype(BF16), st.astype(BF16))
        khat = (kk_ref[sl, :] * jnp.exp2(bl - bc)).astype(BF16)
        st_ref[...] = st * jnp.exp2(bl) + _dot_tn(iv, khat)
        gt = gt_ref[0, sl, :]
        o_ref[0, sl, :] = (_rms(o, g) * (gt * jax.nn.sigmoid(gt))).astype(o_ref.dtype)

    @pl.when(ti == pl.num_programs(2) - 1)
    def _():
        s_ref[...] = st_ref[...].T


def _hgrn(proj3, s0, lb, g, chunk, tb):
    b, t, _ = proj3.shape
    sub = min(chunk, HG_SUB)
    s0_b = (lambda bi: bi) if s0.shape[0] == b else (lambda bi: 0)
    kern = functools.partial(_hgrn_kernel, chunk=chunk, sub=sub)
    seg = lambda k: pl.BlockSpec((1, tb, HD), lambda bi, h, ti: (bi, ti, k * H + h))
    return pl.pallas_call(
        kern,
        grid=(b, H, t // tb),
        in_specs=[
            seg(3), seg(4), seg(5), seg(6),
            pl.BlockSpec((None, None, HD, HD), lambda bi, h, ti: (s0_b(bi), h, 0, 0)),
            pl.BlockSpec((1, 1, HD), lambda bi, h, ti: (h, 0, 0)),
            pl.BlockSpec((1, HD), lambda bi, h, ti: (0, 0)),
        ],
        out_specs=[
            pl.BlockSpec((1, tb, HD), lambda bi, h, ti: (bi, ti, h)),
            pl.BlockSpec((None, None, HD, HD), lambda bi, h, ti: (bi, h, 0, 0)),
        ],
        out_shape=[jax.ShapeDtypeStruct((b, t, SEG), BF16), jax.ShapeDtypeStruct((b, H, HD, HD), F32)],
        scratch_shapes=[pltpu.VMEM((HD, HD), F32), pltpu.VMEM((tb, HD), F32), pltpu.VMEM((tb, HD), F32),
                        pltpu.VMEM((tb, HD), F32)],
        compiler_params=_cp(("parallel", "parallel", "arbitrary")),
        name="hgrn",
    )(proj3, proj3, proj3, proj3, s0, lb, g.reshape(1, HD))


def _outproj_router_kernel(att_ref, hg_ref, x_ref, wo_ref, g_ref, wr_ref, br_ref, cin_ref,
                           h_ref, xn_ref, ti_ref, gate_ref, rank_ref, cout_ref, tri_ref, cnt_ref, *, tm):
    i = pl.program_id(0)

    @pl.when(i == 0)
    def _():
        r = lax.broadcasted_iota(jnp.int32, (tm, tm), 0)
        c = lax.broadcasted_iota(jnp.int32, (tm, tm), 1)
        tri_ref[...] = jnp.where(c < r, 1.0, 0.0).astype(BF16)
        cnt_ref[...] = cin_ref[...]

    half = att_ref.shape[1]
    mix = _dot(att_ref[...], wo_ref[:half, :]) + _dot(hg_ref[...], wo_ref[half:, :])
    h = x_ref[...] + mix
    h_ref[...] = h
    xn = _rms(h, g_ref[...])
    xn_ref[...] = xn
    xn_hi = xn.astype(BF16)
    xn_lo = (xn - xn_hi.astype(F32)).astype(BF16)
    t = _dot(xn_hi, wr_ref[...])
    logits = (t[:, :N_EXPERTS] + t[:, N_EXPERTS:] + _dot(xn_lo, wr_ref[:, :N_EXPERTS])) + br_ref[...]

    lane = lax.broadcasted_iota(jnp.int32, logits.shape, 1).astype(F32)
    k_lane = lax.broadcasted_iota(jnp.int32, (tm, TOP_K), 1)
    work = logits
    sel = jnp.zeros(logits.shape, F32)
    top_v, top_i = [], []
    for _ in range(TOP_K):
        mx = jnp.max(work, axis=-1, keepdims=True)
        idx = jnp.min(jnp.where(work == mx, lane, float(N_EXPERTS)), axis=-1, keepdims=True)
        hit = lane == idx
        sel = jnp.where(hit, 1.0, sel)
        work = jnp.where(hit, -jnp.inf, work)
        top_v.append(mx)
        top_i.append(idx)
    ex = [jnp.exp(v - top_v[0]) for v in top_v]
    den = ex[0] + ex[1] + ex[2] + ex[3]
    before = _dot(tri_ref[...], sel.astype(BF16)) + cnt_ref[...]
    ti_out = jnp.zeros((tm, TOP_K), jnp.int32)
    gate_out = jnp.zeros((tm, TOP_K), F32)
    rank_out = jnp.zeros((tm, TOP_K), jnp.int32)
    for k in range(TOP_K):
        rk = jnp.sum(jnp.where(lane == top_i[k], before, 0.0), axis=-1, keepdims=True).astype(jnp.int32)
        ti_out = jnp.where(k_lane == k, top_i[k].astype(jnp.int32), ti_out)
        gate_out = jnp.where(k_lane == k, ex[k] / den, gate_out)
        rank_out = jnp.where(k_lane == k, rk, rank_out)
    ti_ref[...] = ti_out
    gate_ref[...] = gate_out
    rank_ref[...] = rank_out
    cnt_ref[...] = cnt_ref[...] + jnp.sum(sel, axis=0, keepdims=True)
    cout_ref[...] = cnt_ref[...]


def _outproj_router(att, hg, x, wo_bf16, g, wr2, br, cnt_in, tm):
    n, d = x.shape
    half = att.shape[1]
    kern = functools.partial(_outproj_router_kernel, tm=tm)
    row = lambda w: pl.BlockSpec((tm, w), lambda i: (i, 0))
    full = lambda a, b: pl.BlockSpec((a, b), lambda i: (0, 0))
    return pl.pallas_call(
        kern,
        grid=(n // tm,),
        in_specs=[row(half), row(half), row(d), full(2 * half, d), full(1, d), full(d, 2 * N_EXPERTS),
                  full(1, N_EXPERTS), full(1, N_EXPERTS)],
        out_specs=[row(d), row(d), row(TOP_K), row(TOP_K), row(TOP_K), full(1, N_EXPERTS)],
        out_shape=[jax.ShapeDtypeStruct((n, d), F32), jax.ShapeDtypeStruct((n, d), F32),
                   jax.ShapeDtypeStruct((n, TOP_K), jnp.int32), jax.ShapeDtypeStruct((n, TOP_K), F32),
                   jax.ShapeDtypeStruct((n, TOP_K), jnp.int32), jax.ShapeDtypeStruct((1, N_EXPERTS), F32)],
        scratch_shapes=[pltpu.VMEM((tm, tm), BF16), pltpu.VMEM((1, N_EXPERTS), F32)],
        compiler_params=_cp(("arbitrary",)),
        name="outproj_router",
    )(att, hg, x, wo_bf16, g.reshape(1, d), wr2, br.reshape(1, N_EXPERTS), cnt_in)


def _row_copy(src_ref, dst_ref, s, d, sem):
    return pltpu.make_async_copy(src_ref.at[pl.ds(s, 1), :], dst_ref.at[pl.ds(d, 1), :], sem)


def _dispatch_kernel(pos_ref, x_ref, xs_in_ref, xs_ref, sem, *, tm):
    del xs_in_ref

    def start(r, c):
        for k in range(TOP_K):
            _row_copy(x_ref, xs_ref, r, pos_ref[r * TOP_K + k], sem).start()
        return c

    lax.fori_loop(0, tm, start, 0)

    def wait(r, c):
        for k in range(TOP_K):
            _row_copy(x_ref, xs_ref, r, pos_ref[r * TOP_K + k], sem).wait()
        return c

    lax.fori_loop(0, tm, wait, 0)


def _dispatch(pos, xn, xs, tm):
    n, d = xn.shape
    kern = functools.partial(_dispatch_kernel, tm=tm)
    return pl.pallas_call(
        kern,
        grid=(n // tm,),
        in_specs=[
            pl.BlockSpec((tm * TOP_K,), lambda i: (i,), memory_space=pltpu.SMEM),
            pl.BlockSpec((tm, d), lambda i: (i, 0)),
            pl.BlockSpec(memory_space=pl.ANY),
        ],
        out_specs=pl.BlockSpec(memory_space=pl.ANY),
        out_shape=jax.ShapeDtypeStruct(xs.shape, xs.dtype),
        scratch_shapes=[pltpu.SemaphoreType.DMA],
        input_output_aliases={2: 0},
        compiler_params=_cp(("arbitrary",)),
        name="dispatch",
    )(pos, xn, xs)


EXPERT_TM = 1024
EXPERT_SUB = 512
EXPERT_TF = 256


def _expert_kernel(te_ref, ns_ref, x_ref, wg_ref, wu_ref, bg_ref, bu_ref, wd_ref, bd_ref, y_ref, xb_ref):
    del te_ref
    i, f = pl.program_id(0), pl.program_id(1)

    @pl.when(f == 0)
    def _():
        xb_ref[...] = x_ref[...].astype(BF16)
        y_ref[...] = jnp.broadcast_to(bd_ref[0], y_ref.shape)

    for sb in range(EXPERT_TM // EXPERT_SUB):
        @pl.when(sb < ns_ref[i])
        def _():
            rows = slice(sb * EXPERT_SUB, (sb + 1) * EXPERT_SUB)
            x = xb_ref[rows, :]
            gate = jnp.minimum(_dot(x, wg_ref[...].astype(BF16)) + bg_ref[0], SWIGLU_LIMIT)
            up = jnp.clip(_dot(x, wu_ref[...].astype(BF16)) + bu_ref[0], -SWIGLU_LIMIT, SWIGLU_LIMIT)
            act = (up + 1.0) * gate * jax.nn.sigmoid(SWIGLU_ALPHA * gate)
            y_ref[rows, :] += _dot(act.astype(BF16), wd_ref[...].astype(BF16))


def _experts(tile_expert, n_sub, xs, w_gate_up, b_gate_up, w_down, b_down):
    slots, d = xs.shape
    dff = w_down.shape[1]
    tm, tf = EXPERT_TM, EXPERT_TF
    nf = dff // tf
    grid_spec = pltpu.PrefetchScalarGridSpec(
        num_scalar_prefetch=2,
        grid=(slots // tm, nf),
        in_specs=[
            pl.BlockSpec((tm, d), lambda i, f, te, ns: (i, 0)),
            pl.BlockSpec((None, d, tf), lambda i, f, te, ns: (te[i], 0, f)),
            pl.BlockSpec((None, d, tf), lambda i, f, te, ns: (te[i], 0, nf + f)),
            pl.BlockSpec((None, 1, tf), lambda i, f, te, ns: (te[i], 0, f)),
            pl.BlockSpec((None, 1, tf), lambda i, f, te, ns: (te[i], 0, nf + f)),
            pl.BlockSpec((None, tf, d), lambda i, f, te, ns: (te[i], f, 0)),
            pl.BlockSpec((None, 1, d), lambda i, f, te, ns: (te[i], 0, 0)),
        ],
        out_specs=pl.BlockSpec((tm, d), lambda i, f, te, ns: (i, 0)),
        scratch_shapes=[pltpu.VMEM((tm, d), BF16)],
    )
    return pl.pallas_call(
        _expert_kernel,
        grid_spec=grid_spec,
        out_shape=jax.ShapeDtypeStruct((slots, d), F32),
        compiler_params=_cp(("arbitrary", "arbitrary")),
        name="experts",
    )(tile_expert, n_sub, xs, w_gate_up, w_gate_up, b_gate_up, b_gate_up, w_down, b_down)


def _combine_kernel(pos_ref, h_ref, gate_ref, g_ref, ys_ref, o_ref, buf_ref, sem, *, tm):
    def start(r, c):
        for k in range(TOP_K):
            _row_copy(ys_ref, buf_ref.at[k], pos_ref[r * TOP_K + k], r, sem).start()
        return c

    lax.fori_loop(0, tm, start, 0)

    def wait(r, c):
        for k in range(TOP_K):
            _row_copy(ys_ref, buf_ref.at[k], pos_ref[r * TOP_K + k], r, sem).wait()
        return c

    lax.fori_loop(0, tm, wait, 0)

    gates = gate_ref[...]
    out = h_ref[...]
    for k in range(TOP_K):
        out = out + gates[:, k:k + 1] * buf_ref[k]
    o_ref[...] = _rms(out, g_ref[...])


def _combine(pos, h, gates, g, ys, tm):
    n, d = h.shape
    kern = functools.partial(_combine_kernel, tm=tm)
    return pl.pallas_call(
        kern,
        grid=(n // tm,),
        in_specs=[
            pl.BlockSpec((tm * TOP_K,), lambda i: (i,), memory_space=pltpu.SMEM),
            pl.BlockSpec((tm, d), lambda i: (i, 0)),
            pl.BlockSpec((tm, TOP_K), lambda i: (i, 0)),
            pl.BlockSpec((1, d), lambda i: (0, 0)),
            pl.BlockSpec(memory_space=pl.ANY),
        ],
        out_specs=pl.BlockSpec((tm, d), lambda i: (i, 0)),
        out_shape=jax.ShapeDtypeStruct((n, d), F32),
        scratch_shapes=[pltpu.VMEM((TOP_K, tm, d), F32), pltpu.SemaphoreType.DMA],
        compiler_params=_cp(("arbitrary",)),
        name="combine",
    )(pos, h, gates, g.reshape(1, d), ys)


TOKEN_TM = 256


def kernel(x_prompt, x_sample, cache_k, cache_v, state_hgrn, page_table, meta_tokens, norm_mix_g, w_in, lambda_q1, lambda_k1, lambda_q2, lambda_k2, attn_sub_g, hgrn_gamma, hgrn_norm_g, w_out, norm_ffn_g, w_router, b_router, w_gate_up, b_gate_up, w_down, b_down, norm_final_g):
    l = 0
    bp, tp, d = x_prompt.shape
    bs, ts, _ = x_sample.shape
    n_p, n_s = bp * tp, bs * ts

    lam_init = 0.8 - 0.6 * math.exp(-0.3 * l)
    lam = (jnp.exp(jnp.sum(lambda_q1[l].astype(F32) * lambda_k1[l].astype(F32)))
           - jnp.exp(jnp.sum(lambda_q2[l].astype(F32) * lambda_k2[l].astype(F32))) + lam_init).reshape(1)
    sub_scale = 1.0 - lam_init
    lb = jnp.cumsum(jax.nn.softmax(hgrn_gamma.astype(F32), axis=0), axis=0)[l].reshape(H, 1, HD)
    w_in_b = w_in[l].astype(BF16)
    w_out_b = w_out[l].astype(BF16)
    wr_hi = w_router[l].astype(BF16)
    wr_lo = (w_router[l] - wr_hi.astype(F32)).astype(BF16)
    wr2 = jnp.concatenate([wr_hi, wr_lo], axis=1)

    proj_p = _norm_matmul(x_prompt.reshape(n_p, d), norm_mix_g[l], w_in_b, 1024, SEG)
    extra = jnp.concatenate([meta_tokens.astype(F32), x_sample.reshape(n_s, d)], axis=0)
    proj_e = _norm_matmul(extra, norm_mix_g[l], w_in_b, extra.shape[0], SEG)
    proj_m, proj_s = proj_e[:N_META], proj_e[N_META:]
    proj_p3 = proj_p.reshape(bp, tp, N_SEG * SEG)

    att_p = _attn_prompt(lam, proj_p3, proj_m, attn_sub_g[l], sub_scale)
    zero_state = jnp.zeros((1, H, HD, HD), F32)
    _, s_meta = _hgrn(proj_m.reshape(1, N_META, N_SEG * SEG), zero_state, lb, hgrn_norm_g[l], N_META, N_META)
    hg_p, s_prompt = _hgrn(proj_p3, s_meta, lb, hgrn_norm_g[l], HG_CHUNK, 512)

    rows_s = lambda k: proj_s[:, k * SEG:(k + 1) * SEG].reshape(bs, ts * H, HD)
    att_s = _attn_decode(lam, page_table, rows_s(0), rows_s(1), rows_s(2), cache_k[l], cache_v[l],
                         attn_sub_g[l], sub_scale)
    hg_s, s_sample = _hgrn(proj_s.reshape(bs, ts, N_SEG * SEG), state_hgrn[l].astype(F32), lb, hgrn_norm_g[l], ts, ts)

    cnt0 = jnp.zeros((1, N_EXPERTS), F32)
    h_p, xn_p, ti_p, gate_p, rank_p, cnt1 = _outproj_router(
        att_p.reshape(n_p, SEG), hg_p.reshape(n_p, SEG), x_prompt.reshape(n_p, d), w_out_b, norm_ffn_g[l],
        wr2, b_router[l], cnt0, TOKEN_TM)
    h_s, xn_s, ti_s, gate_s, rank_s, cnt2 = _outproj_router(
        att_s.reshape(n_s, SEG), hg_s.reshape(n_s, SEG), x_sample.reshape(n_s, d), w_out_b, norm_ffn_g[l],
        wr2, b_router[l], cnt1, TOKEN_TM)

    n_tok = n_p + n_s
    n_tiles = (n_tok * TOP_K) // EXPERT_TM + N_EXPERTS
    counts = cnt2[0].astype(jnp.int32)
    tiles_per = (counts + EXPERT_TM - 1) // EXPERT_TM
    tile_end = jnp.cumsum(tiles_per)
    tile_start = tile_end - tiles_per
    offsets = tile_start * EXPERT_TM
    tile_ids = jnp.arange(n_tiles, dtype=jnp.int32)
    tile_expert = jnp.minimum(jnp.sum((tile_end[None, :] <= tile_ids[:, None]).astype(jnp.int32), axis=1),
                              N_EXPERTS - 1)
    rows_in_tile = jnp.clip(counts[tile_expert] - (tile_ids - tile_start[tile_expert]) * EXPERT_TM, 0, EXPERT_TM)
    rows_in_tile = jnp.where(tile_ids < tile_end[-1], rows_in_tile, 0)
    n_sub = (rows_in_tile + EXPERT_SUB - 1) // EXPERT_SUB
    last_expert = tile_expert[jnp.maximum(tile_end[-1] - 1, 0)]
    tile_expert = jnp.where(tile_ids < tile_end[-1], tile_expert, last_expert)
    pos_p = (offsets[ti_p] + rank_p).reshape(-1)
    pos_s = (offsets[ti_s] + rank_s).reshape(-1)

    xs = jnp.zeros((n_tiles * EXPERT_TM, d), F32)
    xs = _dispatch(pos_p, xn_p, xs, TOKEN_TM)
    xs = _dispatch(pos_s, xn_s, xs, TOKEN_TM)
    ys = _experts(tile_expert, n_sub, xs, w_gate_up[l], b_gate_up[l].reshape(N_EXPERTS, 1, 2 * D_FF),
                  w_down[l], b_down[l].reshape(N_EXPERTS, 1, d))
    y_p = _combine(pos_p, h_p, gate_p, norm_final_g, ys, TOKEN_TM)
    y_s = _combine(pos_s, h_s, gate_s, norm_final_g, ys, TOKEN_TM)

    def kv_prompt(k):
        own = proj_p3[:, :, k * SEG:(k + 1) * SEG]
        meta = jnp.broadcast_to(proj_m[None, :, k * SEG:(k + 1) * SEG], (bp, N_META, SEG))
        return jnp.concatenate([meta, own], axis=1).reshape(1, bp, N_META + tp, H, HD)

    k_sample = proj_s[:, SEG:2 * SEG].reshape(1, bs, ts, H, HD)
    v_sample = proj_s[:, 2 * SEG:3 * SEG].reshape(1, bs, ts, H, HD)
    return (y_p.reshape(bp, tp, d), y_s.reshape(bs, ts, d), kv_prompt(1), kv_prompt(2), s_prompt[None],
            k_sample, v_sample, s_sample[None].astype(state_hgrn.dtype))
```

```python
import functools
import math

import jax
import jax.numpy as jnp
from jax import lax
from jax.experimental import pallas as pl
from jax.experimental.pallas import tpu as pltpu

F32 = jnp.float32
BF16 = jnp.bfloat16

D_MODEL = 2048
N_META = 16
H = 8
HD = 128
DK_ATT = 64
SEG = H * HD
N_SEG = 7
ATT_SCALE = DK_ATT ** -0.5
HG_CHUNK = 64
HG_SUB = 16
HG_UNROLL = 8
N_EXPERTS = 32
TOP_K = 4
D_FF = D_MODEL
SWIGLU_LIMIT = 7.0
SWIGLU_ALPHA = 1.702
NORM_EPS = 1e-5
NEG_INF = -1e30
LOG2E = 1.4426950408889634
PAGE = 128

VMEM_LIMIT = 56 * 1024 * 1024


def _cp(sem, vmem=VMEM_LIMIT):
    return pltpu.CompilerParams(dimension_semantics=sem, vmem_limit_bytes=vmem)


def _dot(a, b):
    return jnp.dot(a, b, preferred_element_type=F32)


def _dot_nt(a, b):
    return lax.dot_general(a, b, (((1,), (1,)), ((), ())), preferred_element_type=F32)


def _dot_tn(a, b):
    return lax.dot_general(a, b, (((0,), (0,)), ((), ())), preferred_element_type=F32)


def _rms(x, g):
    return x * lax.rsqrt(jnp.mean(x * x, axis=-1, keepdims=True) + NORM_EPS) * g


def _norm_matmul_kernel(x_ref, g_ref, w_ref, o_ref, xn_ref):
    @pl.when(pl.program_id(1) == 0)
    def _():
        xn_ref[...] = _rms(x_ref[...], g_ref[...]).astype(BF16)

    o_ref[...] = _dot(xn_ref[...], w_ref[...])


def _norm_matmul(x, g, w_bf16, tm, tn):
    n, d = x.shape
    m = w_bf16.shape[1]
    return pl.pallas_call(
        _norm_matmul_kernel,
        grid=(n // tm, m // tn),
        in_specs=[
            pl.BlockSpec((tm, d), lambda i, j: (i, 0)),
            pl.BlockSpec((1, d), lambda i, j: (0, 0)),
            pl.BlockSpec((d, tn), lambda i, j: (0, j)),
        ],
        out_specs=pl.BlockSpec((tm, tn), lambda i, j: (i, j)),
        out_shape=jax.ShapeDtypeStruct((n, m), F32),
        scratch_shapes=[pltpu.VMEM((tm, d), BF16)],
        compiler_params=_cp(("parallel", "arbitrary")),
        name="norm_inproj",
    )(x, g.reshape(1, d), w_bf16)


ATT_HEADS_PER_STEP = 4


def _attn_prompt_kernel(lam_ref, q_ref, k_ref, v_ref, km_ref, vm_ref, g_ref, o_ref, kb_ref, vt_ref, kmb_ref, vmt_ref,
                        acc_ref, s_ref, *, tq, scale):
    qi = pl.program_id(2)
    heads = range(ATT_HEADS_PER_STEP)
    lanes = lambda hh: slice(hh * HD, (hh + 1) * HD)

    @pl.when(qi == 0)
    def _():
        for hh in heads:
            kb_ref[hh] = k_ref[0, :, lanes(hh)].astype(BF16)
            vt_ref[hh] = v_ref[0, :, lanes(hh)].T.astype(BF16)
            kmb_ref[hh] = km_ref[:, lanes(hh)].astype(BF16)
            vmt_ref[hh] = vm_ref[:, lanes(hh)].T.astype(BF16)

    def split_q(hh):
        qt = (q_ref[0, :, lanes(hh)] * (ATT_SCALE * LOG2E)).T
        d_idx = lax.broadcasted_iota(jnp.int32, qt.shape, 0)
        return jnp.concatenate([jnp.where(d_idx < DK_ATT, qt, 0.0), jnp.where(d_idx >= DK_ATT, qt, 0.0)],
                               axis=1).astype(BF16)

    q2t = [split_q(hh) for hh in heads]

    def update(hh, s, m, l, vt_blk):
        m_new = jnp.maximum(m, jnp.max(s, axis=0, keepdims=True))
        alpha = jnp.exp2(m - m_new)
        p = jnp.exp2(s - m_new)
        l = alpha * l + jnp.sum(p, axis=0, keepdims=True)
        acc_ref[hh] = acc_ref[hh] * alpha + _dot(vt_blk, p.astype(BF16))
        return m_new, l

    def scores(hh, kj):
        off = pl.multiple_of(kj * tq, tq)
        return _dot(kb_ref[hh, pl.ds(off, tq), :], q2t[hh])

    carry = []
    for hh in heads:
        s = _dot(kmb_ref[hh], q2t[hh])
        m = jnp.max(s, axis=0, keepdims=True)
        p = jnp.exp2(s - m)
        carry += [m, jnp.sum(p, axis=0, keepdims=True)]
        acc_ref[hh] = _dot(vmt_ref[hh], p.astype(BF16))
        s_ref[hh] = scores(hh, 0)

    def body(kj, carry):
        off = pl.multiple_of(kj * tq, tq)
        out = []
        for hh in heads:
            s_cur = s_ref[hh]
            s_ref[hh] = scores(hh, kj + 1)
            out += update(hh, s_cur, carry[2 * hh], carry[2 * hh + 1], vt_ref[hh, :, pl.ds(off, tq)])
        return tuple(out)

    carry = lax.fori_loop(0, qi, body, tuple(carry))

    off = pl.multiple_of(qi * tq, tq)
    key = lax.broadcasted_iota(jnp.int32, (tq, tq), 0)
    qry = lax.broadcasted_iota(jnp.int32, (tq, tq), 1)
    causal = key <= qry
    causal2 = jnp.concatenate([causal, causal], axis=1)
    for hh in heads:
        s = jnp.where(causal2, s_ref[hh], NEG_INF)
        _, l = update(hh, s, carry[2 * hh], carry[2 * hh + 1], vt_ref[hh, :, pl.ds(off, tq)])
        o = acc_ref[hh] / l
        att = (o[:, :tq] - lam_ref[0] * o[:, tq:]).T
        o_ref[0, :, lanes(hh)] = (_rms(att, g_ref[...]) * scale).astype(o_ref.dtype)


def _attn_prompt(lam, proj3, projm, g, scale, tq=256):
    b, t, _ = proj3.shape
    hp = ATT_HEADS_PER_STEP
    w = hp * HD
    nh = H // hp
    kern = functools.partial(_attn_prompt_kernel, tq=tq, scale=scale)
    return pl.pallas_call(
        kern,
        grid=(b, nh, t // tq),
        in_specs=[
            pl.BlockSpec(memory_space=pltpu.SMEM),
            pl.BlockSpec((1, tq, w), lambda bi, h, qi: (bi, qi, h)),
            pl.BlockSpec((1, t, w), lambda bi, h, qi: (bi, 0, nh + h)),
            pl.BlockSpec((1, t, w), lambda bi, h, qi: (bi, 0, 2 * nh + h)),
            pl.BlockSpec((N_META, w), lambda bi, h, qi: (0, nh + h)),
            pl.BlockSpec((N_META, w), lambda bi, h, qi: (0, 2 * nh + h)),
            pl.BlockSpec((1, HD), lambda bi, h, qi: (0, 0)),
        ],
        out_specs=pl.BlockSpec((1, tq, w), lambda bi, h, qi: (bi, qi, h)),
        out_shape=jax.ShapeDtypeStruct((b, t, SEG), BF16),
        scratch_shapes=[pltpu.VMEM((hp, t, HD), BF16), pltpu.VMEM((hp, HD, t), BF16),
                        pltpu.VMEM((hp, N_META, HD), BF16), pltpu.VMEM((hp, HD, N_META), BF16),
                        pltpu.VMEM((hp, HD, 2 * tq), F32), pltpu.VMEM((hp, tq, 2 * tq), F32)],
        compiler_params=_cp(("parallel", "parallel", "arbitrary")),
        name="attn_prompt",
    )(lam, proj3, proj3, proj3, projm, projm, g.reshape(1, HD))


DEC_PAGES = 8


def _split_q(q):
    lane = lax.broadcasted_iota(jnp.int32, q.shape, 1)
    q = q * (ATT_SCALE * LOG2E)
    return jnp.concatenate([jnp.where(lane < DK_ATT, q, 0.0), jnp.where(lane >= DK_ATT, q, 0.0)], axis=0).astype(BF16)


def _attn_decode_kernel(pt_ref, lam_ref, q_ref, kn_ref, vn_ref, ck_ref, cv_ref, g_ref, o_ref,
                        kbuf_ref, vbuf_ref, sem, m_ref, l_ref, acc_ref, bias_ref, *, n_tok, n_pages, scale):
    bi, gi = pl.program_id(0), pl.program_id(1)
    n_groups = pl.num_programs(1)
    step = bi * n_groups + gi
    slot = step % 2
    rows = 2 * n_tok * H

    def page_copies(st, sl):
        out = []
        for j in range(DEC_PAGES):
            page = pt_ref[st * DEC_PAGES + j]
            out.append(pltpu.make_async_copy(ck_ref.at[page], kbuf_ref.at[sl, j], sem.at[0, sl]))
            out.append(pltpu.make_async_copy(cv_ref.at[page], vbuf_ref.at[sl, j], sem.at[1, sl]))
        return out

    @pl.when(step == 0)
    def _():
        for c in page_copies(step, slot):
            c.start()
        r = lax.broadcasted_iota(jnp.int32, bias_ref.shape, 0)
        c = lax.broadcasted_iota(jnp.int32, bias_ref.shape, 1)
        bias_ref[...] = jnp.where((r % H) == (c % H), 0.0, NEG_INF).astype(F32)

    @pl.when(step + 1 < pl.num_programs(0) * n_groups)
    def _():
        for c in page_copies(step + 1, 1 - slot):
            c.start()

    @pl.when(gi == 0)
    def _():
        m_ref[...] = jnp.full(m_ref.shape, NEG_INF, F32)
        l_ref[...] = jnp.zeros(l_ref.shape, F32)
        acc_ref[...] = jnp.zeros(acc_ref.shape, F32)

    q2 = _split_q(q_ref[0])

    def step_update(s, vblk):
        m = m_ref[...]
        m_new = jnp.maximum(m, jnp.max(s, axis=-1, keepdims=True))
        alpha = jnp.exp2(m - m_new)
        p = jnp.exp2(s - m_new)
        l_ref[...] = alpha * l_ref[...] + jnp.sum(p, axis=-1, keepdims=True)
        acc_ref[...] = alpha * acc_ref[...] + _dot(p.astype(BF16), vblk)
        m_ref[...] = m_new

    for c in page_copies(step, slot):
        c.wait()
    nkeys = DEC_PAGES * PAGE * H
    kp = kbuf_ref[slot].reshape(nkeys, HD).astype(BF16)
    vp = vbuf_ref[slot].reshape(nkeys, HD).astype(BF16)
    step_update(_dot_nt(q2, kp) + bias_ref[...], vp)

    @pl.when(gi == n_groups - 1)
    def _():
        nk = n_tok * H
        s = _dot_nt(q2, kn_ref[0].astype(BF16)) + bias_ref[:, :nk]
        r = lax.broadcasted_iota(jnp.int32, (rows, nk), 0)
        c = lax.broadcasted_iota(jnp.int32, (rows, nk), 1)
        s = jnp.where((c // H) <= ((r // H) % n_tok), s, NEG_INF)
        step_update(s, vn_ref[0].astype(BF16))
        o = acc_ref[...] / l_ref[...]
        half = rows // 2
        att = o[:half] - lam_ref[0] * o[half:]
        o_ref[0] = (_rms(att, g_ref[...]) * scale).astype(o_ref.dtype)


def _attn_decode(lam, page_table, q, kn, vn, cache_k, cache_v, g, scale):
    b, th, _ = q.shape
    n_tok = th // H
    n_pages = page_table.shape[1]
    rows = 2 * th
    kern = functools.partial(_attn_decode_kernel, n_tok=n_tok, n_pages=n_pages, scale=scale)
    grid_spec = pltpu.PrefetchScalarGridSpec(
        num_scalar_prefetch=1,
        grid=(b, n_pages // DEC_PAGES),
        in_specs=[
            pl.BlockSpec(memory_space=pltpu.SMEM),
            pl.BlockSpec((1, th, HD), lambda bi, p, pt: (bi, 0, 0)),
            pl.BlockSpec((1, th, HD), lambda bi, p, pt: (bi, 0, 0)),
            pl.BlockSpec((1, th, HD), lambda bi, p, pt: (bi, 0, 0)),
            pl.BlockSpec(memory_space=pl.ANY),
            pl.BlockSpec(memory_space=pl.ANY),
            pl.BlockSpec((1, HD), lambda bi, p, pt: (0, 0)),
        ],
        out_specs=pl.BlockSpec((1, th, HD), lambda bi, p, pt: (bi, 0, 0)),
        scratch_shapes=[pltpu.VMEM((2, DEC_PAGES, PAGE, H, HD), F32), pltpu.VMEM((2, DEC_PAGES, PAGE, H, HD), F32),
                        pltpu.SemaphoreType.DMA((2, 2)),
                        pltpu.VMEM((rows, 1), F32), pltpu.VMEM((rows, 1), F32), pltpu.VMEM((rows, HD), F32),
                        pltpu.VMEM((rows, DEC_PAGES * PAGE * H), F32)],
    )
    return pl.pallas_call(
        kern,
        grid_spec=grid_spec,
        out_shape=jax.ShapeDtypeStruct((b, th, HD), BF16),
        compiler_params=_cp(("arbitrary", "arbitrary")),
        name="attn_decode",
    )(page_table.reshape(-1), lam, q, kn, vn, cache_k, cache_v, g.reshape(1, HD))


def _hgrn_intra(q, kk, iv, b, sub):
    c = q.shape[0]
    t_idx = lax.broadcasted_iota(jnp.int32, (sub, HD), 0)
    s_idx = lax.broadcasted_iota(jnp.int32, (sub, HD), 1)
    outs = []
    for r0 in range(0, c, sub):
        bi, qi, ki, vi = b[r0:r0 + sub], q[r0:r0 + sub], kk[r0:r0 + sub], iv[r0:r0 + sub]
        a = jnp.zeros((sub, HD), F32)
        for s in range(sub):
            e = jnp.exp2(bi - bi[s:s + 1])
            a = jnp.where(s_idx == s, jnp.sum(qi * ki[s:s + 1] * e, axis=-1, keepdims=True), a)
        a = jnp.where(s_idx <= t_idx, a, 0.0)
        od = _dot(a[:, :sub].astype(BF16), vi.astype(BF16))
        if r0 > 0:
            ref = b[r0 - 1:r0]
            qs = (qi * jnp.exp2(bi - ref)).astype(BF16)
            ks = (kk[:r0] * jnp.exp2(ref - b[:r0])).astype(BF16)
            od = od + _dot(_dot_nt(qs, ks).astype(BF16), iv[:r0].astype(BF16))
        outs.append(od)
    return jnp.concatenate(outs, axis=0) if len(outs) > 1 else outs[0]


def _hgrn_kernel(q_ref, f_ref, i_ref, gt_ref, s0_ref, lb_ref, g_ref, o_ref, s_ref, st_ref, b_ref, kk_ref, oi_ref,
                 *, chunk, sub):
    ti = pl.program_id(2)
    tb = q_ref.shape[1]
    n_chunks = tb // chunk

    @pl.when(ti == 0)
    def _():
        st_ref[...] = s0_ref[...].T

    lb = lb_ref[0]
    f = lb + (1.0 - lb) * jax.nn.sigmoid(f_ref[0])
    kk_ref[...] = 1.0 - f
    b = jnp.log(f) * LOG2E
    rc = lax.broadcasted_iota(jnp.int32, b.shape, 0) % chunk
    sh = 1
    while sh < chunk:
        b = b + jnp.where(rc >= sh, pltpu.roll(b, sh, axis=0), 0.0)
        sh *= 2
    b_ref[...] = b

    def intra(ci, carry):
        sl = pl.ds(pl.multiple_of(ci * chunk, chunk), chunk)
        oi_ref[sl, :] = _hgrn_intra(q_ref[0, sl, :], kk_ref[sl, :], i_ref[0, sl, :], b_ref[sl, :], sub)
        return carry

    lax.fori_loop(0, n_chunks, intra, 0, unroll=min(n_chunks, HG_UNROLL))

    g = g_ref[...]
    for ci in range(n_chunks):
        sl = slice(ci * chunk, (ci + 1) * chunk)
        bc = b_ref[sl, :]
        bl = bc[chunk - 1:chunk]
        iv = i_ref[0, sl, :].astype(BF16)
        st = st_ref[...]
        o = oi_ref[sl, :] + _dot_nt((q_ref[0, sl, :] * jnp.exp2(bc)).astype(BF16), st.astype(BF16))
        khat = (kk_ref[sl, :] * jnp.exp2(bl - bc)).astype(BF16)
        st_ref[...] = st * jnp.exp2(bl) + _dot_tn(iv, khat)
        gt = gt_ref[0, sl, :]
        o_ref[0, sl, :] = (_rms(o, g) * (gt * jax.nn.sigmoid(gt))).astype(o_ref.dtype)

    @pl.when(ti == pl.num_programs(2) - 1)
    def _():
        s_ref[...] = st_ref[...].T


def _hgrn(proj3, s0, lb, g, chunk, tb):
    b, t, _ = proj3.shape
    sub = min(chunk, HG_SUB)
    s0_b = (lambda bi: bi) if s0.shape[0] == b else (lambda bi: 0)
    kern = functools.partial(_hgrn_kernel, chunk=chunk, sub=sub)
    seg = lambda k: pl.BlockSpec((1, tb, HD), lambda bi, h, ti: (bi, ti, k * H + h))
    return pl.pallas_call(
        kern,
        grid=(b, H, t // tb),
        in_specs=[
            seg(3), seg(4), seg(5), seg(6),
            pl.BlockSpec((None, None, HD, HD), lambda bi, h, ti: (s0_b(bi), h, 0, 0)),
            pl.BlockSpec((1, 1, HD), lambda bi, h, ti: (h, 0, 0)),
            pl.BlockSpec((1, HD), lambda bi, h, ti: (0, 0)),
        ],
        out_specs=[
            pl.BlockSpec((1, tb, HD), lambda bi, h, ti: (bi, ti, h)),
            pl.BlockSpec((None, None, HD, HD), lambda bi, h, ti: (bi, h, 0, 0)),
        ],
        out_shape=[jax.ShapeDtypeStruct((b, t, SEG), BF16), jax.ShapeDtypeStruct((b, H, HD, HD), F32)],
        scratch_shapes=[pltpu.VMEM((HD, HD), F32), pltpu.VMEM((tb, HD), F32), pltpu.VMEM((tb, HD), F32),
                        pltpu.VMEM((tb, HD), F32)],
        compiler_params=_cp(("parallel", "parallel", "arbitrary")),
        name="hgrn",
    )(proj3, proj3, proj3, proj3, s0, lb, g.reshape(1, HD))


def _outproj_router_kernel(att_ref, hg_ref, x_ref, wo_ref, g_ref, wr_ref, br_ref, cin_ref,
                           h_ref, xn_ref, ti_ref, gate_ref, rank_ref, cout_ref, tri_ref, cnt_ref, *, tm):
    i = pl.program_id(0)

    @pl.when(i == 0)
    def _():
        r = lax.broadcasted_iota(jnp.int32, (tm, tm), 0)
        c = lax.broadcasted_iota(jnp.int32, (tm, tm), 1)
        tri_ref[...] = jnp.where(c < r, 1.0, 0.0).astype(BF16)
        cnt_ref[...] = cin_ref[...]

    half = att_ref.shape[1]
    mix = _dot(att_ref[...], wo_ref[:half, :]) + _dot(hg_ref[...], wo_ref[half:, :])
    h = x_ref[...] + mix
    h_ref[...] = h
    xn = _rms(h, g_ref[...])
    xn_ref[...] = xn
    xn_hi = xn.astype(BF16)
    xn_lo = (xn - xn_hi.astype(F32)).astype(BF16)
    t = _dot(xn_hi, wr_ref[...])
    logits = (t[:, :N_EXPERTS] + t[:, N_EXPERTS:] + _dot(xn_lo, wr_ref[:, :N_EXPERTS])) + br_ref[...]

    lane = lax.broadcasted_iota(jnp.int32, logits.shape, 1).astype(F32)
    k_lane = lax.broadcasted_iota(jnp.int32, (tm, TOP_K), 1)
    work = logits
    sel = jnp.zeros(logits.shape, F32)
    top_v, top_i = [], []
    for _ in range(TOP_K):
        mx = jnp.max(work, axis=-1, keepdims=True)
        idx = jnp.min(jnp.where(work == mx, lane, float(N_EXPERTS)), axis=-1, keepdims=True)
        hit = lane == idx
        sel = jnp.where(hit, 1.0, sel)
        work = jnp.where(hit, -jnp.inf, work)
        top_v.append(mx)
        top_i.append(idx)
    ex = [jnp.exp(v - top_v[0]) for v in top_v]
    den = ex[0] + ex[1] + ex[2] + ex[3]
    before = _dot(tri_ref[...], sel.astype(BF16)) + cnt_ref[...]
    ti_out = jnp.zeros((tm, TOP_K), jnp.int32)
    gate_out = jnp.zeros((tm, TOP_K), F32)
    rank_out = jnp.zeros((tm, TOP_K), jnp.int32)
    for k in range(TOP_K):
        rk = jnp.sum(jnp.where(lane == top_i[k], before, 0.0), axis=-1, keepdims=True).astype(jnp.int32)
        ti_out = jnp.where(k_lane == k, top_i[k].astype(jnp.int32), ti_out)
        gate_out = jnp.where(k_lane == k, ex[k] / den, gate_out)
        rank_out = jnp.where(k_lane == k, rk, rank_out)
    ti_ref[...] = ti_out
    gate_ref[...] = gate_out
    rank_ref[...] = rank_out
    cnt_ref[...] = cnt_ref[...] + jnp.sum(sel, axis=0, keepdims=True)
    cout_ref[...] = cnt_ref[...]


def _outproj_router(att, hg, x, wo_bf16, g, wr2, br, cnt_in, tm):
    n, d = x.shape
    half = att.shape[1]
    kern = functools.partial(_outproj_router_kernel, tm=tm)
    row = lambda w: pl.BlockSpec((tm, w), lambda i: (i, 0))
    full = lambda a, b: pl.BlockSpec((a, b), lambda i: (0, 0))
    return pl.pallas_call(
        kern,
        grid=(n // tm,),
        in_specs=[row(half), row(half), row(d), full(2 * half, d), full(1, d), full(d, 2 * N_EXPERTS),
                  full(1, N_EXPERTS), full(1, N_EXPERTS)],
        out_specs=[row(d), row(d), row(TOP_K), row(TOP_K), row(TOP_K), full(1, N_EXPERTS)],
        out_shape=[jax.ShapeDtypeStruct((n, d), F32), jax.ShapeDtypeStruct((n, d), F32),
                   jax.ShapeDtypeStruct((n, TOP_K), jnp.int32), jax.ShapeDtypeStruct((n, TOP_K), F32),
                   jax.ShapeDtypeStruct((n, TOP_K), jnp.int32), jax.ShapeDtypeStruct((1, N_EXPERTS), F32)],
        scratch_shapes=[pltpu.VMEM((tm, tm), BF16), pltpu.VMEM((1, N_EXPERTS), F32)],
        compiler_params=_cp(("arbitrary",)),
        name="outproj_router",
    )(att, hg, x, wo_bf16, g.reshape(1, d), wr2, br.reshape(1, N_EXPERTS), cnt_in)


def _row_copy(src_ref, dst_ref, s, d, sem):
    return pltpu.make_async_copy(src_ref.at[pl.ds(s, 1), :], dst_ref.at[pl.ds(d, 1), :], sem)


SLOT_ALIGN = 8


def _dispatch_kernel(pos_ref, pad0_ref, padn_ref, xa_ref, xb_ref, xs_ref, sem, zero_ref, zsem, *, tm, n_a_tiles):
    i = pl.program_id(0)
    n_tail = zero_ref.shape[0]
    tail_copy = pltpu.make_async_copy(zero_ref, xs_ref.at[pl.ds(xs_ref.shape[0] - n_tail, n_tail), :], zsem)

    def pad_copy(e, j):
        return _row_copy(zero_ref, xs_ref, 0, pad0_ref[e] + j, zsem)

    def pad_rows(fn):
        def body(e, c):
            for j in range(SLOT_ALIGN - 1):
                @pl.when(j < padn_ref[e])
                def _():
                    fn(pad_copy(e, j))
            return c
        lax.fori_loop(0, N_EXPERTS, body, 0)

    @pl.when(i == 0)
    def _():
        zero_ref[...] = jnp.zeros(zero_ref.shape, zero_ref.dtype)
        tail_copy.start()
        tail_copy.wait()
        pad_rows(lambda c: c.start())
        pad_rows(lambda c: c.wait())

    def move(x_ref):
        def start(r, c):
            for k in range(TOP_K):
                _row_copy(x_ref, xs_ref, r, pos_ref[r * TOP_K + k], sem).start()
            return c

        lax.fori_loop(0, tm, start, 0)

        def wait(r, c):
            for k in range(TOP_K):
                _row_copy(x_ref, xs_ref, r, pos_ref[r * TOP_K + k], sem).wait()
            return c

        lax.fori_loop(0, tm, wait, 0)

    @pl.when(i < n_a_tiles)
    def _():
        move(xa_ref)

    @pl.when(i >= n_a_tiles)
    def _():
        move(xb_ref)


def _dispatch(pos, xa, xb, tm, n_rows, n_tail, pad0, padn):
    na, d = xa.shape
    nb = xb.shape[0]
    n_a_tiles = na // tm
    kern = functools.partial(_dispatch_kernel, tm=tm, n_a_tiles=n_a_tiles)
    return pl.pallas_call(
        kern,
        grid=((na + nb) // tm,),
        in_specs=[
            pl.BlockSpec((tm * TOP_K,), lambda i: (i,), memory_space=pltpu.SMEM),
            pl.BlockSpec(memory_space=pltpu.SMEM),
            pl.BlockSpec(memory_space=pltpu.SMEM),
            pl.BlockSpec((tm, d), lambda i: (jnp.minimum(i, n_a_tiles - 1), 0)),
            pl.BlockSpec((tm, d), lambda i: (jnp.maximum(i - n_a_tiles, 0), 0)),
        ],
        out_specs=pl.BlockSpec(memory_space=pl.ANY),
        out_shape=jax.ShapeDtypeStruct((n_rows, d), xa.dtype),
        scratch_shapes=[pltpu.SemaphoreType.DMA, pltpu.VMEM((n_tail, d), xa.dtype), pltpu.SemaphoreType.DMA],
        compiler_params=_cp(("arbitrary",)),
        name="dispatch",
    )(pos, pad0, padn, xa, xb)


EXPERT_T = 2304
EXPERT_SUB = 768
EXPERT_STG = 384
EXPERT_TF = 256


def _expert_kernel(te_ref, row0_ref, ns_ref, xs_ref, wg_ref, wu_ref, bg_ref, bu_ref, wd_ref, bd_ref, ys_ref,
                   xb_ref, acc_ref, stg_ref, wgb_ref, wub_ref, wdb_ref, sem_in, sem_out, *, n_defined):
    del te_ref
    i, f = pl.program_id(0), pl.program_id(1)
    n_f = pl.num_programs(1)
    sub, stg = EXPERT_SUB, EXPERT_STG
    n_sub = EXPERT_T // sub
    per_sub = sub // stg
    ns, r0 = ns_ref[i], row0_ref[i]

    def x_copy(p):
        src = xs_ref.at[pl.ds(pl.multiple_of(r0 + p * stg, SLOT_ALIGN), stg), :]
        return pltpu.make_async_copy(src, stg_ref.at[p % 2], sem_in.at[p % 2])

    def y_copy(item_r0, j):
        dst = ys_ref.at[pl.ds(pl.multiple_of(item_r0 + j * sub, SLOT_ALIGN), sub), :]
        return pltpu.make_async_copy(acc_ref.at[pl.ds(j * sub, sub), :], dst, sem_out)

    @pl.when((i == 0) & (f == 0))
    def _():
        n_tail = ys_ref.shape[0] - n_defined
        chunk = min(n_tail, EXPERT_T)
        acc_ref[0:chunk, :] = jnp.zeros((chunk, acc_ref.shape[1]), F32)
        for c0 in range(0, n_tail, chunk):
            c0 = min(c0, n_tail - chunk)
            tail = pltpu.make_async_copy(acc_ref.at[pl.ds(0, chunk), :], ys_ref.at[pl.ds(n_defined + c0, chunk), :],
                                         sem_out)
            tail.start()
            tail.wait()

    @pl.when(f == 0)
    def _():
        ip = jnp.maximum(i - 1, 0)
        prev_ns = jnp.where(i > 0, ns_ref[ip], 0)
        prev_r0 = row0_ref[ip]
        for j in range(n_sub):
            @pl.when(j < prev_ns)
            def _():
                y_copy(prev_r0, j).wait()

        @pl.when(ns > 0)
        def _():
            x_copy(0).start()
            x_copy(1).start()

    @pl.when(ns > 0)
    def _():
        wgb_ref[...] = wg_ref[...].astype(BF16)
        wub_ref[...] = wu_ref[...].astype(BF16)
        wdb_ref[...] = wd_ref[...].astype(BF16)

    for j in range(n_sub):
        @pl.when(j < ns)
        def _():
            rows = slice(j * sub, (j + 1) * sub)

            @pl.when(f == 0)
            def _():
                for p in range(j * per_sub, (j + 1) * per_sub):
                    x_copy(p).wait()
                    xb_ref[p * stg:(p + 1) * stg, :] = stg_ref[p % 2].astype(BF16)
                    if p + 2 < n_sub * per_sub:
                        @pl.when(p + 2 < ns * per_sub)
                        def _():
                            x_copy(p + 2).start()
                acc_ref[rows, :] = jnp.broadcast_to(bd_ref[0], (sub, acc_ref.shape[1]))

            x = xb_ref[rows, :]
            gate = jnp.minimum(_dot(x, wgb_ref[...]) + bg_ref[0], SWIGLU_LIMIT)
            up = jnp.clip(_dot(x, wub_ref[...]) + bu_ref[0], -SWIGLU_LIMIT, SWIGLU_LIMIT)
            act = (up + 1.0) * gate * jax.nn.sigmoid(SWIGLU_ALPHA * gate)
            acc_ref[rows, :] += _dot(act.astype(BF16), wdb_ref[...])

            @pl.when(f == n_f - 1)
            def _():
                y_copy(r0, j).start()

    @pl.when((i == pl.num_programs(0) - 1) & (f == n_f - 1))
    def _():
        for j in range(n_sub):
            @pl.when(j < ns)
            def _():
                y_copy(r0, j).wait()


def _experts(item_expert, item_row0, item_nsub, xs, w_gate_up, b_gate_up, w_down, b_down, n_defined):
    rows, d = xs.shape
    dff = w_down.shape[1]
    tf = EXPERT_TF
    nf = dff // tf
    n_items = item_expert.shape[0]
    grid_spec = pltpu.PrefetchScalarGridSpec(
        num_scalar_prefetch=3,
        grid=(n_items, nf),
        in_specs=[
            pl.BlockSpec(memory_space=pl.ANY),
            pl.BlockSpec((None, d, tf), lambda i, f, te, r0, ns: (te[i], 0, f)),
            pl.BlockSpec((None, d, tf), lambda i, f, te, r0, ns: (te[i], 0, nf + f)),
            pl.BlockSpec((None, 1, tf), lambda i, f, te, r0, ns: (te[i], 0, f)),
            pl.BlockSpec((None, 1, tf), lambda i, f, te, r0, ns: (te[i], 0, nf + f)),
            pl.BlockSpec((None, tf, d), lambda i, f, te, r0, ns: (te[i], f, 0)),
            pl.BlockSpec((None, 1, d), lambda i, f, te, r0, ns: (te[i], 0, 0)),
        ],
        out_specs=pl.BlockSpec(memory_space=pl.ANY),
        scratch_shapes=[pltpu.VMEM((EXPERT_T, d), BF16), pltpu.VMEM((EXPERT_T, d), F32),
                        pltpu.VMEM((2, EXPERT_STG, d), F32),
                        pltpu.VMEM((d, tf), BF16), pltpu.VMEM((d, tf), BF16), pltpu.VMEM((tf, d), BF16),
                        pltpu.SemaphoreType.DMA((2,)),
                        pltpu.SemaphoreType.DMA],
    )
    return pl.pallas_call(
        functools.partial(_expert_kernel, n_defined=n_defined),
        grid_spec=grid_spec,
        out_shape=jax.ShapeDtypeStruct((rows, d), F32),
        compiler_params=_cp(("arbitrary", "arbitrary")),
        name="experts",
    )(item_expert, item_row0, item_nsub, xs, w_gate_up, w_gate_up, b_gate_up, b_gate_up, w_down, b_down)


def _combine_kernel(pos_ref, h_ref, gate_ref, g_ref, ys_ref, o_ref, buf_ref, sem, *, tm):
    def start(r, c):
        for k in range(TOP_K):
            _row_copy(ys_ref, buf_ref.at[k], pos_ref[r * TOP_K + k], r, sem).start()
        return c

    lax.fori_loop(0, tm, start, 0)

    def wait(r, c):
        for k in range(TOP_K):
            _row_copy(ys_ref, buf_ref.at[k], pos_ref[r * TOP_K + k], r, sem).wait()
        return c

    lax.fori_loop(0, tm, wait, 0)

    gates = gate_ref[...]
    out = h_ref[...]
    for k in range(TOP_K):
        out = out + gates[:, k:k + 1] * buf_ref[k]
    o_ref[...] = _rms(out, g_ref[...])


def _combine(pos, h, gates, g, ys, tm):
    n, d = h.shape
    kern = functools.partial(_combine_kernel, tm=tm)
    return pl.pallas_call(
        kern,
        grid=(n // tm,),
        in_specs=[
            pl.BlockSpec((tm * TOP_K,), lambda i: (i,), memory_space=pltpu.SMEM),
            pl.BlockSpec((tm, d), lambda i: (i, 0)),
            pl.BlockSpec((tm, TOP_K), lambda i: (i, 0)),
            pl.BlockSpec((1, d), lambda i: (0, 0)),
            pl.BlockSpec(memory_space=pl.ANY),
        ],
        out_specs=pl.BlockSpec((tm, d), lambda i: (i, 0)),
        out_shape=jax.ShapeDtypeStruct((n, d), F32),
        scratch_shapes=[pltpu.VMEM((TOP_K, tm, d), F32), pltpu.SemaphoreType.DMA],
        compiler_params=_cp(("arbitrary",)),
        name="combine",
    )(pos, h, gates, g.reshape(1, d), ys)


TOKEN_TM = 256


def kernel(x_prompt, x_sample, cache_k, cache_v, state_hgrn, page_table, meta_tokens, norm_mix_g, w_in, lambda_q1, lambda_k1, lambda_q2, lambda_k2, attn_sub_g, hgrn_gamma, hgrn_norm_g, w_out, norm_ffn_g, w_router, b_router, w_gate_up, b_gate_up, w_down, b_down, norm_final_g):
    l = 0
    bp, tp, d = x_prompt.shape
    bs, ts, _ = x_sample.shape
    n_p, n_s = bp * tp, bs * ts

    lam_init = 0.8 - 0.6 * math.exp(-0.3 * l)
    lam = (jnp.exp(jnp.sum(lambda_q1[l].astype(F32) * lambda_k1[l].astype(F32)))
           - jnp.exp(jnp.sum(lambda_q2[l].astype(F32) * lambda_k2[l].astype(F32))) + lam_init).reshape(1)
    sub_scale = 1.0 - lam_init
    lb = jnp.cumsum(jax.nn.softmax(hgrn_gamma.astype(F32), axis=0), axis=0)[l].reshape(H, 1, HD)
    w_in_b = w_in[l].astype(BF16)
    w_out_b = w_out[l].astype(BF16)
    wr_hi = w_router[l].astype(BF16)
    wr_lo = (w_router[l] - wr_hi.astype(F32)).astype(BF16)
    wr2 = jnp.concatenate([wr_hi, wr_lo], axis=1)

    proj_p = _norm_matmul(x_prompt.reshape(n_p, d), norm_mix_g[l], w_in_b, 1024, SEG)
    extra = jnp.concatenate([meta_tokens.astype(F32), x_sample.reshape(n_s, d)], axis=0)
    proj_e = _norm_matmul(extra, norm_mix_g[l], w_in_b, extra.shape[0], SEG)
    proj_m, proj_s = proj_e[:N_META], proj_e[N_META:]
    proj_p3 = proj_p.reshape(bp, tp, N_SEG * SEG)

    att_p = _attn_prompt(lam, proj_p3, proj_m, attn_sub_g[l], sub_scale)
    zero_state = jnp.zeros((1, H, HD, HD), F32)
    _, s_meta = _hgrn(proj_m.reshape(1, N_META, N_SEG * SEG), zero_state, lb, hgrn_norm_g[l], N_META, N_META)
    hg_p, s_prompt = _hgrn(proj_p3, s_meta, lb, hgrn_norm_g[l], HG_CHUNK, 512)

    rows_s = lambda k: proj_s[:, k * SEG:(k + 1) * SEG].reshape(bs, ts * H, HD)
    att_s = _attn_decode(lam, page_table, rows_s(0), rows_s(1), rows_s(2), cache_k[l], cache_v[l],
                         attn_sub_g[l], sub_scale)
    hg_s, s_sample = _hgrn(proj_s.reshape(bs, ts, N_SEG * SEG), state_hgrn[l].astype(F32), lb, hgrn_norm_g[l], ts, ts)

    cnt0 = jnp.zeros((1, N_EXPERTS), F32)
    h_p, xn_p, ti_p, gate_p, rank_p, cnt1 = _outproj_router(
        att_p.reshape(n_p, SEG), hg_p.reshape(n_p, SEG), x_prompt.reshape(n_p, d), w_out_b, norm_ffn_g[l],
        wr2, b_router[l], cnt0, TOKEN_TM)
    h_s, xn_s, ti_s, gate_s, rank_s, cnt2 = _outproj_router(
        att_s.reshape(n_s, SEG), hg_s.reshape(n_s, SEG), x_sample.reshape(n_s, d), w_out_b, norm_ffn_g[l],
        wr2, b_router[l], cnt1, TOKEN_TM)

    n_slots = (n_p + n_s) * TOP_K + N_EXPERTS * (SLOT_ALIGN - 1)
    n_items = n_slots // EXPERT_T + N_EXPERTS
    counts = cnt2[0].astype(jnp.int32)
    counts_al = (counts + SLOT_ALIGN - 1) // SLOT_ALIGN * SLOT_ALIGN
    offsets = jnp.cumsum(counts_al) - counts_al
    items_per = (counts + EXPERT_T - 1) // EXPERT_T
    item_end = jnp.cumsum(items_per)
    item_start = item_end - items_per
    item_ids = jnp.arange(n_items, dtype=jnp.int32)
    item_valid = item_ids < item_end[-1]
    item_expert = jnp.minimum(jnp.sum((item_end[None, :] <= item_ids[:, None]).astype(jnp.int32), axis=1),
                              N_EXPERTS - 1)
    item_k = item_ids - item_start[item_expert]
    item_rows = jnp.where(item_valid, jnp.clip(counts[item_expert] - item_k * EXPERT_T, 0, EXPERT_T), 0)
    item_row0 = jnp.where(item_valid, offsets[item_expert] + item_k * EXPERT_T, 0)
    item_nsub = (item_rows + EXPERT_SUB - 1) // EXPERT_SUB
    last_expert = item_expert[jnp.maximum(item_end[-1] - 1, 0)]
    item_expert = jnp.where(item_valid, item_expert, last_expert)
    pos_p = (offsets[ti_p] + rank_p).reshape(-1)
    pos_s = (offsets[ti_s] + rank_s).reshape(-1)

    n_routed = (n_p + n_s) * TOP_K
    n_rows = n_slots + EXPERT_SUB
    xs = _dispatch(jnp.concatenate([pos_p, pos_s]), xn_p, xn_s, TOKEN_TM, n_rows, n_rows - n_routed,
                   offsets + counts, counts_al - counts)
    ys = _experts(item_expert, item_row0, item_nsub, xs, w_gate_up[l],
                  b_gate_up[l].reshape(N_EXPERTS, 1, 2 * D_FF), w_down[l], b_down[l].reshape(N_EXPERTS, 1, d),
                  n_routed)
    y_p = _combine(pos_p, h_p, gate_p, norm_final_g, ys, TOKEN_TM)
    y_s = _combine(pos_s, h_s, gate_s, norm_final_g, ys, TOKEN_TM)

    def kv_prompt(k):
        own = proj_p3[:, :, k * SEG:(k + 1) * SEG]
        meta = jnp.broadcast_to(proj_m[None, :, k * SEG:(k + 1) * SEG], (bp, N_META, SEG))
        return jnp.concatenate([meta, own], axis=1).reshape(1, bp, N_META + tp, H, HD)

    k_sample = proj_s[:, SEG:2 * SEG].reshape(1, bs, ts, H, HD)
    v_sample = proj_s[:, 2 * SEG:3 * SEG].reshape(1, bs, ts, H, HD)
    return (y_p.reshape(bp, tp, d), y_s.reshape(bs, ts, d), kv_prompt(1), kv_prompt(2), s_prompt[None],
            k_sample, v_sample, s_sample[None].astype(state_hgrn.dtype))
```

```python
import functools
import math

import jax
import jax.numpy as jnp
from jax import lax
from jax.experimental import pallas as pl
from jax.experimental.pallas import tpu as pltpu

F32 = jnp.float32
BF16 = jnp.bfloat16

D_MODEL = 2048
N_META = 16
H = 8
HD = 128
DK_ATT = 64
SEG = H * HD
N_SEG = 7
ATT_SCALE = DK_ATT ** -0.5
HG_CHUNK = 64
HG_SUB = 16
HG_UNROLL = 8
N_EXPERTS = 32
TOP_K = 4
D_FF = D_MODEL
SWIGLU_LIMIT = 7.0
SWIGLU_ALPHA = 1.702
NORM_EPS = 1e-5
NEG_INF = -1e30
LOG2E = 1.4426950408889634
PAGE = 128

VMEM_LIMIT = 56 * 1024 * 1024


def _cp(sem, vmem=VMEM_LIMIT):
    return pltpu.CompilerParams(dimension_semantics=sem, vmem_limit_bytes=vmem)


def _dot(a, b):
    return jnp.dot(a, b, preferred_element_type=F32)


def _dot_nt(a, b):
    return lax.dot_general(a, b, (((1,), (1,)), ((), ())), preferred_element_type=F32)


def _dot_tn(a, b):
    return lax.dot_general(a, b, (((0,), (0,)), ((), ())), preferred_element_type=F32)


def _rms(x, g):
    return x * lax.rsqrt(jnp.mean(x * x, axis=-1, keepdims=True) + NORM_EPS) * g


def _norm_matmul_kernel(x_ref, g_ref, w_ref, o_ref, xn_ref):
    @pl.when(pl.program_id(1) == 0)
    def _():
        xn_ref[...] = _rms(x_ref[...], g_ref[...]).astype(BF16)

    o_ref[...] = _dot(xn_ref[...], w_ref[...])


def _norm_matmul(x, g, w_bf16, tm, tn):
    n, d = x.shape
    m = w_bf16.shape[1]
    return pl.pallas_call(
        _norm_matmul_kernel,
        grid=(n // tm, m // tn),
        in_specs=[
            pl.BlockSpec((tm, d), lambda i, j: (i, 0)),
            pl.BlockSpec((1, d), lambda i, j: (0, 0)),
            pl.BlockSpec((d, tn), lambda i, j: (0, j)),
        ],
        out_specs=pl.BlockSpec((tm, tn), lambda i, j: (i, j)),
        out_shape=jax.ShapeDtypeStruct((n, m), F32),
        scratch_shapes=[pltpu.VMEM((tm, d), BF16)],
        compiler_params=_cp(("parallel", "arbitrary")),
        name="norm_inproj",
    )(x, g.reshape(1, d), w_bf16)


ATT_HEADS_PER_STEP = 4


def _attn_prompt_kernel(lam_ref, q_ref, k_ref, v_ref, km_ref, vm_ref, g_ref, o_ref, kb_ref, vt_ref, kmb_ref, vmt_ref,
                        acc_ref, s_ref, *, tq, scale):
    qi = pl.program_id(2)
    heads = range(ATT_HEADS_PER_STEP)
    lanes = lambda hh: slice(hh * HD, (hh + 1) * HD)

    @pl.when(qi == 0)
    def _():
        for hh in heads:
            kb_ref[hh] = k_ref[0, :, lanes(hh)].astype(BF16)
            vt_ref[hh] = v_ref[0, :, lanes(hh)].T.astype(BF16)
            kmb_ref[hh] = km_ref[:, lanes(hh)].astype(BF16)
            vmt_ref[hh] = vm_ref[:, lanes(hh)].T.astype(BF16)

    def split_q(hh):
        qt = (q_ref[0, :, lanes(hh)] * (ATT_SCALE * LOG2E)).T
        d_idx = lax.broadcasted_iota(jnp.int32, qt.shape, 0)
        return jnp.concatenate([jnp.where(d_idx < DK_ATT, qt, 0.0), jnp.where(d_idx >= DK_ATT, qt, 0.0)],
                               axis=1).astype(BF16)

    q2t = [split_q(hh) for hh in heads]

    def update(hh, s, m, l, vt_blk):
        m_new = jnp.maximum(m, jnp.max(s, axis=0, keepdims=True))
        alpha = jnp.exp2(m - m_new)
        p = jnp.exp2(s - m_new)
        l = alpha * l + jnp.sum(p, axis=0, keepdims=True)
        acc_ref[hh] = acc_ref[hh] * alpha + _dot(vt_blk, p.astype(BF16))
        return m_new, l

    def scores(hh, kj):
        off = pl.multiple_of(kj * tq, tq)
        return _dot(kb_ref[hh, pl.ds(off, tq), :], q2t[hh])

    carry = []
    for hh in heads:
        s = _dot(kmb_ref[hh], q2t[hh])
        m = jnp.max(s, axis=0, keepdims=True)
        p = jnp.exp2(s - m)
        carry += [m, jnp.sum(p, axis=0, keepdims=True)]
        acc_ref[hh] = _dot(vmt_ref[hh], p.astype(BF16))
        s_ref[hh] = scores(hh, 0)

    def body(kj, carry):
        off = pl.multiple_of(kj * tq, tq)
        out = []
        for hh in heads:
            s_cur = s_ref[hh]
            s_ref[hh] = scores(hh, kj + 1)
            out += update(hh, s_cur, carry[2 * hh], carry[2 * hh + 1], vt_ref[hh, :, pl.ds(off, tq)])
        return tuple(out)

    carry = lax.fori_loop(0, qi, body, tuple(carry))

    off = pl.multiple_of(qi * tq, tq)
    key = lax.broadcasted_iota(jnp.int32, (tq, tq), 0)
    qry = lax.broadcasted_iota(jnp.int32, (tq, tq), 1)
    causal = key <= qry
    causal2 = jnp.concatenate([causal, causal], axis=1)
    for hh in heads:
        s = jnp.where(causal2, s_ref[hh], NEG_INF)
        _, l = update(hh, s, carry[2 * hh], carry[2 * hh + 1], vt_ref[hh, :, pl.ds(off, tq)])
        o = acc_ref[hh] / l
        att = (o[:, :tq] - lam_ref[0] * o[:, tq:]).T
        o_ref[0, :, lanes(hh)] = (_rms(att, g_ref[...]) * scale).astype(o_ref.dtype)


def _attn_prompt(lam, proj3, projm, g, scale, tq=256):
    b, t, _ = proj3.shape
    hp = ATT_HEADS_PER_STEP
    w = hp * HD
    nh = H // hp
    kern = functools.partial(_attn_prompt_kernel, tq=tq, scale=scale)
    return pl.pallas_call(
        kern,
        grid=(b, nh, t // tq),
        in_specs=[
            pl.BlockSpec(memory_space=pltpu.SMEM),
            pl.BlockSpec((1, tq, w), lambda bi, h, qi: (bi, qi, h)),
            pl.BlockSpec((1, t, w), lambda bi, h, qi: (bi, 0, nh + h)),
            pl.BlockSpec((1, t, w), lambda bi, h, qi: (bi, 0, 2 * nh + h)),
            pl.BlockSpec((N_META, w), lambda bi, h, qi: (0, nh + h)),
            pl.BlockSpec((N_META, w), lambda bi, h, qi: (0, 2 * nh + h)),
            pl.BlockSpec((1, HD), lambda bi, h, qi: (0, 0)),
        ],
        out_specs=pl.BlockSpec((1, tq, w), lambda bi, h, qi: (bi, qi, h)),
        out_shape=jax.ShapeDtypeStruct((b, t, SEG), BF16),
        scratch_shapes=[pltpu.VMEM((hp, t, HD), BF16), pltpu.VMEM((hp, HD, t), BF16),
                        pltpu.VMEM((hp, N_META, HD), BF16), pltpu.VMEM((hp, HD, N_META), BF16),
                        pltpu.VMEM((hp, HD, 2 * tq), F32), pltpu.VMEM((hp, tq, 2 * tq), F32)],
        compiler_params=_cp(("parallel", "parallel", "arbitrary")),
        name="attn_prompt",
    )(lam, proj3, proj3, proj3, projm, projm, g.reshape(1, HD))


DEC_PAGES = 8


def _split_q(q):
    lane = lax.broadcasted_iota(jnp.int32, q.shape, 1)
    q = q * (ATT_SCALE * LOG2E)
    return jnp.concatenate([jnp.where(lane < DK_ATT, q, 0.0), jnp.where(lane >= DK_ATT, q, 0.0)], axis=0).astype(BF16)


def _attn_decode_kernel(pt_ref, lam_ref, q_ref, kn_ref, vn_ref, ck_ref, cv_ref, g_ref, o_ref,
                        kbuf_ref, vbuf_ref, sem, m_ref, l_ref, acc_ref, bias_ref, *, n_tok, n_pages, scale):
    bi, gi = pl.program_id(0), pl.program_id(1)
    n_groups = pl.num_programs(1)
    step = bi * n_groups + gi
    slot = step % 2
    rows = 2 * n_tok * H

    def page_copies(st, sl):
        out = []
        for j in range(DEC_PAGES):
            page = pt_ref[st * DEC_PAGES + j]
            out.append(pltpu.make_async_copy(ck_ref.at[page], kbuf_ref.at[sl, j], sem.at[0, sl]))
            out.append(pltpu.make_async_copy(cv_ref.at[page], vbuf_ref.at[sl, j], sem.at[1, sl]))
        return out

    @pl.when(step == 0)
    def _():
        for c in page_copies(step, slot):
            c.start()
        r = lax.broadcasted_iota(jnp.int32, bias_ref.shape, 0)
        c = lax.broadcasted_iota(jnp.int32, bias_ref.shape, 1)
        bias_ref[...] = jnp.where((r % H) == (c % H), 0.0, NEG_INF).astype(F32)

    @pl.when(step + 1 < pl.num_programs(0) * n_groups)
    def _():
        for c in page_copies(step + 1, 1 - slot):
            c.start()

    @pl.when(gi == 0)
    def _():
        m_ref[...] = jnp.full(m_ref.shape, NEG_INF, F32)
        l_ref[...] = jnp.zeros(l_ref.shape, F32)
        acc_ref[...] = jnp.zeros(acc_ref.shape, F32)

    q2 = _split_q(q_ref[0])

    def step_update(s, vblk):
        m = m_ref[...]
        m_new = jnp.maximum(m, jnp.max(s, axis=-1, keepdims=True))
        alpha = jnp.exp2(m - m_new)
        p = jnp.exp2(s - m_new)
        l_ref[...] = alpha * l_ref[...] + jnp.sum(p, axis=-1, keepdims=True)
        acc_ref[...] = alpha * acc_ref[...] + _dot(p.astype(BF16), vblk)
        m_ref[...] = m_new

    for c in page_copies(step, slot):
        c.wait()
    nkeys = DEC_PAGES * PAGE * H
    kp = kbuf_ref[slot].reshape(nkeys, HD).astype(BF16)
    vp = vbuf_ref[slot].reshape(nkeys, HD).astype(BF16)
    step_update(_dot_nt(q2, kp) + bias_ref[...], vp)

    @pl.when(gi == n_groups - 1)
    def _():
        nk = n_tok * H
        s = _dot_nt(q2, kn_ref[0].astype(BF16)) + bias_ref[:, :nk]
        r = lax.broadcasted_iota(jnp.int32, (rows, nk), 0)
        c = lax.broadcasted_iota(jnp.int32, (rows, nk), 1)
        s = jnp.where((c // H) <= ((r // H) % n_tok), s, NEG_INF)
        step_update(s, vn_ref[0].astype(BF16))
        o = acc_ref[...] / l_ref[...]
        half = rows // 2
        att = o[:half] - lam_ref[0] * o[half:]
        o_ref[0] = (_rms(att, g_ref[...]) * scale).astype(o_ref.dtype)


def _attn_decode(lam, page_table, q, kn, vn, cache_k, cache_v, g, scale):
    b, th, _ = q.shape
    n_tok = th // H
    n_pages = page_table.shape[1]
    rows = 2 * th
    kern = functools.partial(_attn_decode_kernel, n_tok=n_tok, n_pages=n_pages, scale=scale)
    grid_spec = pltpu.PrefetchScalarGridSpec(
        num_scalar_prefetch=1,
        grid=(b, n_pages // DEC_PAGES),
        in_specs=[
            pl.BlockSpec(memory_space=pltpu.SMEM),
            pl.BlockSpec((1, th, HD), lambda bi, p, pt: (bi, 0, 0)),
            pl.BlockSpec((1, th, HD), lambda bi, p, pt: (bi, 0, 0)),
            pl.BlockSpec((1, th, HD), lambda bi, p, pt: (bi, 0, 0)),
            pl.BlockSpec(memory_space=pl.ANY),
            pl.BlockSpec(memory_space=pl.ANY),
            pl.BlockSpec((1, HD), lambda bi, p, pt: (0, 0)),
        ],
        out_specs=pl.BlockSpec((1, th, HD), lambda bi, p, pt: (bi, 0, 0)),
        scratch_shapes=[pltpu.VMEM((2, DEC_PAGES, PAGE, H, HD), F32), pltpu.VMEM((2, DEC_PAGES, PAGE, H, HD), F32),
                        pltpu.SemaphoreType.DMA((2, 2)),
                        pltpu.VMEM((rows, 1), F32), pltpu.VMEM((rows, 1), F32), pltpu.VMEM((rows, HD), F32),
                        pltpu.VMEM((rows, DEC_PAGES * PAGE * H), F32)],
    )
    return pl.pallas_call(
        kern,
        grid_spec=grid_spec,
        out_shape=jax.ShapeDtypeStruct((b, th, HD), BF16),
        compiler_params=_cp(("arbitrary", "arbitrary")),
        name="attn_decode",
    )(page_table.reshape(-1), lam, q, kn, vn, cache_k, cache_v, g.reshape(1, HD))


def _hgrn_intra(q, kk, iv, b, sub):
    c = q.shape[0]
    t_idx = lax.broadcasted_iota(jnp.int32, (sub, HD), 0)
    s_idx = lax.broadcasted_iota(jnp.int32, (sub, HD), 1)
    outs = []
    for r0 in range(0, c, sub):
        bi, qi, ki, vi = b[r0:r0 + sub], q[r0:r0 + sub], kk[r0:r0 + sub], iv[r0:r0 + sub]
        a = jnp.zeros((sub, HD), F32)
        for s in range(sub):
            e = jnp.exp2(bi - bi[s:s + 1])
            a = jnp.where(s_idx == s, jnp.sum(qi * ki[s:s + 1] * e, axis=-1, keepdims=True), a)
        a = jnp.where(s_idx <= t_idx, a, 0.0)
        od = _dot(a[:, :sub].astype(BF16), vi.astype(BF16))
        if r0 > 0:
            ref = b[r0 - 1:r0]
            qs = (qi * jnp.exp2(bi - ref)).astype(BF16)
            ks = (kk[:r0] * jnp.exp2(ref - b[:r0])).astype(BF16)
            od = od + _dot(_dot_nt(qs, ks).astype(BF16), iv[:r0].astype(BF16))
        outs.append(od)
    return jnp.concatenate(outs, axis=0) if len(outs) > 1 else outs[0]


def _hgrn_kernel(q_ref, f_ref, i_ref, gt_ref, s0_ref, lb_ref, g_ref, o_ref, s_ref, st_ref, b_ref, kk_ref, oi_ref,
                 *, chunk, sub):
    ti = pl.program_id(2)
    nb, tb = q_ref.shape[0], q_ref.shape[1]
    n_chunks = tb // chunk
    lb = lb_ref[0]
    g = g_ref[...]

    for bb in range(nb):
        @pl.when(ti == 0)
        def _():
            st_ref[bb] = s0_ref[bb if s0_ref.shape[0] == nb else 0].T

        f = lb + (1.0 - lb) * jax.nn.sigmoid(f_ref[bb])
        kk_ref[bb] = 1.0 - f
        b = jnp.log(f) * LOG2E
        rc = lax.broadcasted_iota(jnp.int32, b.shape, 0) % chunk
        sh = 1
        while sh < chunk:
            b = b + jnp.where(rc >= sh, pltpu.roll(b, sh, axis=0), 0.0)
            sh *= 2
        b_ref[bb] = b

        def intra(ci, carry):
            sl = pl.ds(pl.multiple_of(ci * chunk, chunk), chunk)
            oi_ref[bb, sl, :] = _hgrn_intra(q_ref[bb, sl, :], kk_ref[bb, sl, :], i_ref[bb, sl, :], b_ref[bb, sl, :],
                                            sub)
            return carry

        lax.fori_loop(0, n_chunks, intra, 0, unroll=min(n_chunks, HG_UNROLL))

        for ci in range(n_chunks):
            sl = slice(ci * chunk, (ci + 1) * chunk)
            bc = b_ref[bb, sl, :]
            bl = bc[chunk - 1:chunk]
            iv = i_ref[bb, sl, :].astype(BF16)
            st = st_ref[bb]
            o = oi_ref[bb, sl, :] + _dot_nt((q_ref[bb, sl, :] * jnp.exp2(bc)).astype(BF16), st.astype(BF16))
            khat = (kk_ref[bb, sl, :] * jnp.exp2(bl - bc)).astype(BF16)
            st_ref[bb] = st * jnp.exp2(bl) + _dot_tn(iv, khat)
            gt = gt_ref[bb, sl, :]
            o_ref[bb, sl, :] = (_rms(o, g) * (gt * jax.nn.sigmoid(gt))).astype(o_ref.dtype)

        @pl.when(ti == pl.num_programs(2) - 1)
        def _():
            s_ref[bb] = st_ref[bb].T


def _hgrn(proj3, s0, lb, g, chunk, tb, nb=1):
    b, t, _ = proj3.shape
    sub = min(chunk, HG_SUB)
    shared_s0 = s0.shape[0] != b
    kern = functools.partial(_hgrn_kernel, chunk=chunk, sub=sub)
    seg = lambda k: pl.BlockSpec((nb, tb, HD), lambda bi, h, ti: (bi, ti, k * H + h))
    return pl.pallas_call(
        kern,
        grid=(b // nb, H, t // tb),
        in_specs=[
            seg(3), seg(4), seg(5), seg(6),
            pl.BlockSpec((1 if shared_s0 else nb, None, HD, HD), lambda bi, h, ti: (0 if shared_s0 else bi, h, 0, 0)),
            pl.BlockSpec((1, 1, HD), lambda bi, h, ti: (h, 0, 0)),
            pl.BlockSpec((1, HD), lambda bi, h, ti: (0, 0)),
        ],
        out_specs=[
            pl.BlockSpec((nb, tb, HD), lambda bi, h, ti: (bi, ti, h)),
            pl.BlockSpec((nb, None, HD, HD), lambda bi, h, ti: (bi, h, 0, 0)),
        ],
        out_shape=[jax.ShapeDtypeStruct((b, t, SEG), BF16), jax.ShapeDtypeStruct((b, H, HD, HD), F32)],
        scratch_shapes=[pltpu.VMEM((nb, HD, HD), F32), pltpu.VMEM((nb, tb, HD), F32), pltpu.VMEM((nb, tb, HD), F32),
                        pltpu.VMEM((nb, tb, HD), F32)],
        compiler_params=_cp(("parallel", "parallel", "arbitrary")),
        name="hgrn",
    )(proj3, proj3, proj3, proj3, s0, lb, g.reshape(1, HD))


def _outproj_router_kernel(att_ref, hg_ref, x_ref, wo_ref, g_ref, wr_ref, br_ref, cin_ref,
                           h_ref, xn_ref, ti_ref, gate_ref, rank_ref, cout_ref, tri_ref, cnt_ref, *, tm):
    i = pl.program_id(0)

    @pl.when(i == 0)
    def _():
        r = lax.broadcasted_iota(jnp.int32, (tm, tm), 0)
        c = lax.broadcasted_iota(jnp.int32, (tm, tm), 1)
        tri_ref[...] = jnp.where(c < r, 1.0, 0.0).astype(BF16)
        cnt_ref[...] = cin_ref[...]

    half = att_ref.shape[1]
    mix = _dot(att_ref[...], wo_ref[:half, :]) + _dot(hg_ref[...], wo_ref[half:, :])
    h = x_ref[...] + mix
    h_ref[...] = h
    xn = _rms(h, g_ref[...])
    xn_ref[...] = xn
    xn_hi = xn.astype(BF16)
    xn_lo = (xn - xn_hi.astype(F32)).astype(BF16)
    t = _dot(xn_hi, wr_ref[...])
    logits = (t[:, :N_EXPERTS] + t[:, N_EXPERTS:] + _dot(xn_lo, wr_ref[:, :N_EXPERTS])) + br_ref[...]

    lane = lax.broadcasted_iota(jnp.int32, logits.shape, 1).astype(F32)
    k_lane = lax.broadcasted_iota(jnp.int32, (tm, TOP_K), 1)
    work = logits
    sel = jnp.zeros(logits.shape, F32)
    top_v, top_i = [], []
    for _ in range(TOP_K):
        mx = jnp.max(work, axis=-1, keepdims=True)
        idx = jnp.min(jnp.where(work == mx, lane, float(N_EXPERTS)), axis=-1, keepdims=True)
        hit = lane == idx
        sel = jnp.where(hit, 1.0, sel)
        work = jnp.where(hit, -jnp.inf, work)
        top_v.append(mx)
        top_i.append(idx)
    ex = [jnp.exp(v - top_v[0]) for v in top_v]
    den = ex[0] + ex[1] + ex[2] + ex[3]
    before = _dot(tri_ref[...], sel.astype(BF16)) + cnt_ref[...]
    ti_out = jnp.zeros((tm, TOP_K), jnp.int32)
    gate_out = jnp.zeros((tm, TOP_K), F32)
    rank_out = jnp.zeros((tm, TOP_K), jnp.int32)
    for k in range(TOP_K):
        rk = jnp.sum(jnp.where(lane == top_i[k], before, 0.0), axis=-1, keepdims=True).astype(jnp.int32)
        ti_out = jnp.where(k_lane == k, top_i[k].astype(jnp.int32), ti_out)
        gate_out = jnp.where(k_lane == k, ex[k] / den, gate_out)
        rank_out = jnp.where(k_lane == k, rk, rank_out)
    ti_ref[...] = ti_out
    gate_ref[...] = gate_out
    rank_ref[...] = rank_out
    cnt_ref[...] = cnt_ref[...] + jnp.sum(sel, axis=0, keepdims=True)
    cout_ref[...] = cnt_ref[...]


def _outproj_router(att, hg, x, wo_bf16, g, wr2, br, cnt_in, tm):
    n, d = x.shape
    half = att.shape[1]
    kern = functools.partial(_outproj_router_kernel, tm=tm)
    row = lambda w: pl.BlockSpec((tm, w), lambda i: (i, 0))
    full = lambda a, b: pl.BlockSpec((a, b), lambda i: (0, 0))
    return pl.pallas_call(
        kern,
        grid=(n // tm,),
        in_specs=[row(half), row(half), row(d), full(2 * half, d), full(1, d), full(d, 2 * N_EXPERTS),
                  full(1, N_EXPERTS), full(1, N_EXPERTS)],
        out_specs=[row(d), row(d), row(TOP_K), row(TOP_K), row(TOP_K), full(1, N_EXPERTS)],
        out_shape=[jax.ShapeDtypeStruct((n, d), F32), jax.ShapeDtypeStruct((n, d), F32),
                   jax.ShapeDtypeStruct((n, TOP_K), jnp.int32), jax.ShapeDtypeStruct((n, TOP_K), F32),
                   jax.ShapeDtypeStruct((n, TOP_K), jnp.int32), jax.ShapeDtypeStruct((1, N_EXPERTS), F32)],
        scratch_shapes=[pltpu.VMEM((tm, tm), BF16), pltpu.VMEM((1, N_EXPERTS), F32)],
        compiler_params=_cp(("arbitrary",)),
        name="outproj_router",
    )(att, hg, x, wo_bf16, g.reshape(1, d), wr2, br.reshape(1, N_EXPERTS), cnt_in)


def _row_copy(src_ref, dst_ref, s, d, sem):
    return pltpu.make_async_copy(src_ref.at[pl.ds(s, 1), :], dst_ref.at[pl.ds(d, 1), :], sem)


SLOT_ALIGN = 8


def _dispatch_kernel(pos_ref, pad0_ref, padn_ref, xa_ref, xb_ref, xs_ref, sem, zero_ref, zsem, *, tm, n_a_tiles):
    i = pl.program_id(0)
    n_tail = zero_ref.shape[0]
    tail_copy = pltpu.make_async_copy(zero_ref, xs_ref.at[pl.ds(xs_ref.shape[0] - n_tail, n_tail), :], zsem)

    def pad_copy(e, j):
        return _row_copy(zero_ref, xs_ref, 0, pad0_ref[e] + j, zsem)

    def pad_rows(fn):
        def body(e, c):
            for j in range(SLOT_ALIGN - 1):
                @pl.when(j < padn_ref[e])
                def _():
                    fn(pad_copy(e, j))
            return c
        lax.fori_loop(0, N_EXPERTS, body, 0)

    @pl.when(i == 0)
    def _():
        zero_ref[...] = jnp.zeros(zero_ref.shape, zero_ref.dtype)
        tail_copy.start()
        tail_copy.wait()
        pad_rows(lambda c: c.start())
        pad_rows(lambda c: c.wait())

    def move(x_ref):
        def start(r, c):
            for k in range(TOP_K):
                _row_copy(x_ref, xs_ref, r, pos_ref[r * TOP_K + k], sem).start(priority=k % 2)
            return c

        lax.fori_loop(0, tm, start, 0)

        def wait(r, c):
            for k in range(TOP_K):
                _row_copy(x_ref, xs_ref, r, pos_ref[r * TOP_K + k], sem).wait()
            return c

        lax.fori_loop(0, tm, wait, 0)

    @pl.when(i < n_a_tiles)
    def _():
        move(xa_ref)

    @pl.when(i >= n_a_tiles)
    def _():
        move(xb_ref)


def _dispatch(pos, xa, xb, tm, n_rows, n_tail, pad0, padn):
    na, d = xa.shape
    nb = xb.shape[0]
    n_a_tiles = na // tm
    kern = functools.partial(_dispatch_kernel, tm=tm, n_a_tiles=n_a_tiles)
    return pl.pallas_call(
        kern,
        grid=((na + nb) // tm,),
        in_specs=[
            pl.BlockSpec((tm * TOP_K,), lambda i: (i,), memory_space=pltpu.SMEM),
            pl.BlockSpec(memory_space=pltpu.SMEM),
            pl.BlockSpec(memory_space=pltpu.SMEM),
            pl.BlockSpec((tm, d), lambda i: (jnp.minimum(i, n_a_tiles - 1), 0)),
            pl.BlockSpec((tm, d), lambda i: (jnp.maximum(i - n_a_tiles, 0), 0)),
        ],
        out_specs=pl.BlockSpec(memory_space=pl.ANY),
        out_shape=jax.ShapeDtypeStruct((n_rows, d), xa.dtype),
        scratch_shapes=[pltpu.SemaphoreType.DMA, pltpu.VMEM((n_tail, d), xa.dtype), pltpu.SemaphoreType.DMA],
        compiler_params=_cp(("arbitrary",)),
        name="dispatch",
    )(pos, pad0, padn, xa, xb)


EXPERT_T = 2304
EXPERT_SUB = 768
EXPERT_STG = 384
EXPERT_TF = 256


def _expert_kernel(te_ref, row0_ref, ns_ref, xs_ref, wg_ref, wu_ref, bg_ref, bu_ref, wd_ref, bd_ref, ys_ref,
                   xb_ref, act_ref, stg_ref, ybuf_ref, wgb_ref, wub_ref, wdb_ref, sem_in, sem_out, *, n_defined):
    del te_ref
    i, s = pl.program_id(0), pl.program_id(1)
    nf = act_ref.shape[0]
    tf = act_ref.shape[2]
    sub, stg = EXPERT_SUB, EXPERT_STG
    n_sub = EXPERT_T // sub
    per_sub = sub // stg
    ns, r0 = ns_ref[i], row0_ref[i]
    in_a = s < nf
    slot = s % 2

    def x_copy(p):
        src = xs_ref.at[pl.ds(pl.multiple_of(r0 + p * stg, SLOT_ALIGN), stg), :]
        return pltpu.make_async_copy(src, stg_ref.at[p % 2], sem_in.at[p % 2])

    def y_copy(j, sl):
        col = pl.ds(pl.multiple_of((s - nf) * tf, tf), tf)
        dst = ys_ref.at[pl.ds(pl.multiple_of(r0 + j * sub, SLOT_ALIGN), sub), col]
        return pltpu.make_async_copy(ybuf_ref.at[sl, j], dst, sem_out.at[sl, j])

    @pl.when((i == 0) & (s == 0))
    def _():
        n_tail = ys_ref.shape[0] - n_defined
        chunk = min(n_tail, stg)
        stg_ref[0, 0:chunk, :] = jnp.zeros((chunk, stg_ref.shape[2]), F32)
        for c0 in range(0, n_tail, chunk):
            c0 = min(c0, n_tail - chunk)
            tail = pltpu.make_async_copy(stg_ref.at[0, pl.ds(0, chunk), :],
                                         ys_ref.at[pl.ds(n_defined + c0, chunk), :], sem_in.at[0])
            tail.start()
            tail.wait()

    @pl.when((s == 0) & (ns > 0))
    def _():
        x_copy(0).start()
        x_copy(1).start()

    @pl.when(in_a & (ns > 0))
    def _():
        wgb_ref[...] = wg_ref[...].astype(BF16)
        wub_ref[...] = wu_ref[...].astype(BF16)

    @pl.when(jnp.logical_not(in_a) & (ns > 0))
    def _():
        wdb_ref[...] = wd_ref[...].astype(BF16)

    for j in range(n_sub):
        rows = slice(j * sub, (j + 1) * sub)

        @pl.when(in_a & (j < ns))
        def _():
            @pl.when(s == 0)
            def _():
                for p in range(j * per_sub, (j + 1) * per_sub):
                    x_copy(p).wait()
                    xb_ref[p * stg:(p + 1) * stg, :] = stg_ref[p % 2].astype(BF16)
                    if p + 2 < n_sub * per_sub:
                        @pl.when(p + 2 < ns * per_sub)
                        def _():
                            x_copy(p + 2).start()

            x = xb_ref[rows, :]
            gate = jnp.minimum(_dot(x, wgb_ref[...]) + bg_ref[0], SWIGLU_LIMIT)
            up = jnp.clip(_dot(x, wub_ref[...]) + bu_ref[0], -SWIGLU_LIMIT, SWIGLU_LIMIT)
            act = (up + 1.0) * gate * jax.nn.sigmoid(SWIGLU_ALPHA * gate)
            act_ref[jnp.minimum(s, nf - 1), rows, :] = act.astype(BF16)

        @pl.when(jnp.logical_not(in_a) & (j < ns))
        def _():
            @pl.when(s >= nf + 2)
            def _():
                y_copy(j, slot).wait()

            a = jnp.concatenate([act_ref[f, rows, :] for f in range(nf)], axis=1)
            ybuf_ref[slot, j] = _dot(a, wdb_ref[...]) + bd_ref[0]
            y_copy(j, slot).start()

            @pl.when(s == 2 * nf - 1)
            def _():
                y_copy(j, 1 - slot).wait()
                y_copy(j, slot).wait()


def _experts(item_expert, item_row0, item_nsub, xs, w_gate_up, b_gate_up, w_down, b_down, n_defined):
    rows, d = xs.shape
    dff = w_down.shape[1]
    tf = EXPERT_TF
    nf = dff // tf
    n_items = item_expert.shape[0]
    fa = lambda s: jnp.minimum(s, nf - 1)
    cb = lambda s: jnp.maximum(s - nf, 0)
    grid_spec = pltpu.PrefetchScalarGridSpec(
        num_scalar_prefetch=3,
        grid=(n_items, 2 * nf),
        in_specs=[
            pl.BlockSpec(memory_space=pl.ANY),
            pl.BlockSpec((None, d, tf), lambda i, s, te, r0, ns: (te[i], 0, fa(s))),
            pl.BlockSpec((None, d, tf), lambda i, s, te, r0, ns: (te[i], 0, nf + fa(s))),
            pl.BlockSpec((None, 1, tf), lambda i, s, te, r0, ns: (te[i], 0, fa(s))),
            pl.BlockSpec((None, 1, tf), lambda i, s, te, r0, ns: (te[i], 0, nf + fa(s))),
            pl.BlockSpec((None, dff, tf), lambda i, s, te, r0, ns: (te[i], 0, cb(s))),
            pl.BlockSpec((None, 1, tf), lambda i, s, te, r0, ns: (te[i], 0, cb(s))),
        ],
        out_specs=pl.BlockSpec(memory_space=pl.ANY),
        scratch_shapes=[pltpu.VMEM((EXPERT_T, d), BF16), pltpu.VMEM((nf, EXPERT_T, tf), BF16),
                        pltpu.VMEM((2, EXPERT_STG, d), F32),
                        pltpu.VMEM((2, EXPERT_T // EXPERT_SUB, EXPERT_SUB, tf), F32),
                        pltpu.VMEM((d, tf), BF16), pltpu.VMEM((d, tf), BF16), pltpu.VMEM((dff, tf), BF16),
                        pltpu.SemaphoreType.DMA((2,)),
                        pltpu.SemaphoreType.DMA((2, EXPERT_T // EXPERT_SUB))],
    )
    return pl.pallas_call(
        functools.partial(_expert_kernel, n_defined=n_defined),
        grid_spec=grid_spec,
        out_shape=jax.ShapeDtypeStruct((rows, d), F32),
        compiler_params=_cp(("arbitrary", "arbitrary")),
        name="experts",
    )(item_expert, item_row0, item_nsub, xs, w_gate_up, w_gate_up, b_gate_up, b_gate_up, w_down, b_down)


def _combine_kernel(pos_ref, h_ref, gate_ref, g_ref, ys_ref, o_ref, buf_ref, sem, *, tm):
    def start(r, c):
        for k in range(TOP_K):
            _row_copy(ys_ref, buf_ref.at[k], pos_ref[r * TOP_K + k], r, sem).start(priority=k % 2)
        return c

    lax.fori_loop(0, tm, start, 0)

    def wait(r, c):
        for k in range(TOP_K):
            _row_copy(ys_ref, buf_ref.at[k], pos_ref[r * TOP_K + k], r, sem).wait()
        return c

    lax.fori_loop(0, tm, wait, 0)

    gates = gate_ref[...]
    out = h_ref[...]
    for k in range(TOP_K):
        out = out + gates[:, k:k + 1] * buf_ref[k]
    o_ref[...] = _rms(out, g_ref[...])


def _combine(pos, h, gates, g, ys, tm):
    n, d = h.shape
    kern = functools.partial(_combine_kernel, tm=tm)
    return pl.pallas_call(
        kern,
        grid=(n // tm,),
        in_specs=[
            pl.BlockSpec((tm * TOP_K,), lambda i: (i,), memory_space=pltpu.SMEM),
            pl.BlockSpec((tm, d), lambda i: (i, 0)),
            pl.BlockSpec((tm, TOP_K), lambda i: (i, 0)),
            pl.BlockSpec((1, d), lambda i: (0, 0)),
            pl.BlockSpec(memory_space=pl.ANY),
        ],
        out_specs=pl.BlockSpec((tm, d), lambda i: (i, 0)),
        out_shape=jax.ShapeDtypeStruct((n, d), F32),
        scratch_shapes=[pltpu.VMEM((TOP_K, tm, d), F32), pltpu.SemaphoreType.DMA],
        compiler_params=_cp(("arbitrary",)),
        name="combine",
    )(pos, h, gates, g.reshape(1, d), ys)


TOKEN_TM = 256


def kernel(x_prompt, x_sample, cache_k, cache_v, state_hgrn, page_table, meta_tokens, norm_mix_g, w_in, lambda_q1, lambda_k1, lambda_q2, lambda_k2, attn_sub_g, hgrn_gamma, hgrn_norm_g, w_out, norm_ffn_g, w_router, b_router, w_gate_up, b_gate_up, w_down, b_down, norm_final_g):
    l = 0
    bp, tp, d = x_prompt.shape
    bs, ts, _ = x_sample.shape
    n_p, n_s = bp * tp, bs * ts

    lam_init = 0.8 - 0.6 * math.exp(-0.3 * l)
    lam = (jnp.exp(jnp.sum(lambda_q1[l].astype(F32) * lambda_k1[l].astype(F32)))
           - jnp.exp(jnp.sum(lambda_q2[l].astype(F32) * lambda_k2[l].astype(F32))) + lam_init).reshape(1)
    sub_scale = 1.0 - lam_init
    lb = jnp.cumsum(jax.nn.softmax(hgrn_gamma.astype(F32), axis=0), axis=0)[l].reshape(H, 1, HD)
    w_in_b = w_in[l].astype(BF16)
    w_out_b = w_out[l].astype(BF16)
    wr_hi = w_router[l].astype(BF16)
    wr_lo = (w_router[l] - wr_hi.astype(F32)).astype(BF16)
    wr2 = jnp.concatenate([wr_hi, wr_lo], axis=1)

    proj_p = _norm_matmul(x_prompt.reshape(n_p, d), norm_mix_g[l], w_in_b, 1024, SEG)
    extra = jnp.concatenate([meta_tokens.astype(F32), x_sample.reshape(n_s, d)], axis=0)
    proj_e = _norm_matmul(extra, norm_mix_g[l], w_in_b, extra.shape[0], SEG)
    proj_m, proj_s = proj_e[:N_META], proj_e[N_META:]
    proj_p3 = proj_p.reshape(bp, tp, N_SEG * SEG)

    att_p = _attn_prompt(lam, proj_p3, proj_m, attn_sub_g[l], sub_scale)
    zero_state = jnp.zeros((1, H, HD, HD), F32)
    _, s_meta = _hgrn(proj_m.reshape(1, N_META, N_SEG * SEG), zero_state, lb, hgrn_norm_g[l], N_META, N_META)
    hg_p, s_prompt = _hgrn(proj_p3, s_meta, lb, hgrn_norm_g[l], HG_CHUNK, 512)

    rows_s = lambda k: proj_s[:, k * SEG:(k + 1) * SEG].reshape(bs, ts * H, HD)
    att_s = _attn_decode(lam, page_table, rows_s(0), rows_s(1), rows_s(2), cache_k[l], cache_v[l],
                         attn_sub_g[l], sub_scale)
    hg_s, s_sample = _hgrn(proj_s.reshape(bs, ts, N_SEG * SEG), state_hgrn[l].astype(F32), lb, hgrn_norm_g[l], ts, ts,
                           nb=8)

    cnt0 = jnp.zeros((1, N_EXPERTS), F32)
    h_p, xn_p, ti_p, gate_p, rank_p, cnt1 = _outproj_router(
        att_p.reshape(n_p, SEG), hg_p.reshape(n_p, SEG), x_prompt.reshape(n_p, d), w_out_b, norm_ffn_g[l],
        wr2, b_router[l], cnt0, TOKEN_TM)
    h_s, xn_s, ti_s, gate_s, rank_s, cnt2 = _outproj_router(
        att_s.reshape(n_s, SEG), hg_s.reshape(n_s, SEG), x_sample.reshape(n_s, d), w_out_b, norm_ffn_g[l],
        wr2, b_router[l], cnt1, TOKEN_TM)

    n_slots = (n_p + n_s) * TOP_K + N_EXPERTS * (SLOT_ALIGN - 1)
    n_items = n_slots // EXPERT_T + N_EXPERTS
    counts = cnt2[0].astype(jnp.int32)
    counts_al = (counts + SLOT_ALIGN - 1) // SLOT_ALIGN * SLOT_ALIGN
    offsets = jnp.cumsum(counts_al) - counts_al
    items_per = (counts + EXPERT_T - 1) // EXPERT_T
    item_end = jnp.cumsum(items_per)
    item_start = item_end - items_per
    item_ids = jnp.arange(n_items, dtype=jnp.int32)
    item_valid = item_ids < item_end[-1]
    item_expert = jnp.minimum(jnp.sum((item_end[None, :] <= item_ids[:, None]).astype(jnp.int32), axis=1),
                              N_EXPERTS - 1)
    item_k = item_ids - item_start[item_expert]
    item_rows = jnp.where(item_valid, jnp.clip(counts[item_expert] - item_k * EXPERT_T, 0, EXPERT_T), 0)
    item_row0 = jnp.where(item_valid, offsets[item_expert] + item_k * EXPERT_T, 0)
    item_nsub = (item_rows + EXPERT_SUB - 1) // EXPERT_SUB
    last_expert = item_expert[jnp.maximum(item_end[-1] - 1, 0)]
    item_expert = jnp.where(item_valid, item_expert, last_expert)
    pos_p = (offsets[ti_p] + rank_p).reshape(-1)
    pos_s = (offsets[ti_s] + rank_s).reshape(-1)

    n_routed = (n_p + n_s) * TOP_K
    n_rows = n_slots + EXPERT_SUB
    xs = _dispatch(jnp.concatenate([pos_p, pos_s]), xn_p, xn_s, TOKEN_TM, n_rows, n_rows - n_routed,
                   offsets + counts, counts_al - counts)
    ys = _experts(item_expert, item_row0, item_nsub, xs, w_gate_up[l],
                  b_gate_up[l].reshape(N_EXPERTS, 1, 2 * D_FF), w_down[l], b_down[l].reshape(N_EXPERTS, 1, d),
                  n_routed)
    y_p = _combine(pos_p, h_p, gate_p, norm_final_g, ys, TOKEN_TM)
    y_s = _combine(pos_s, h_s, gate_s, norm_final_g, ys, TOKEN_TM)

    def kv_prompt(k):
        own = proj_p3[:, :, k * SEG:(k + 1) * SEG]
        meta = jnp.broadcast_to(proj_m[None, :, k * SEG:(k + 1) * SEG], (bp, N_META, SEG))
        return jnp.concatenate([meta, own], axis=1).reshape(1, bp, N_META + tp, H, HD)

    k_sample = proj_s[:, SEG:2 * SEG].reshape(1, bs, ts, H, HD)
    v_sample = proj_s[:, 2 * SEG:3 * SEG].reshape(1, bs, ts, H, HD)
    return (y_p.reshape(bp, tp, d), y_s.reshape(bs, ts, d), kv_prompt(1), kv_prompt(2), s_prompt[None],
            k_sample, v_sample, s_sample[None].astype(state_hgrn.dtype))
```

```python
import functools
import math

import jax
import jax.numpy as jnp
from jax import lax
from jax.experimental import pallas as pl
from jax.experimental.pallas import tpu as pltpu

F32 = jnp.float32
BF16 = jnp.bfloat16

D_MODEL = 2048
N_META = 16
H = 8
HD = 128
DK_ATT = 64
SEG = H * HD
N_SEG = 7
ATT_SCALE = DK_ATT ** -0.5
HG_CHUNK = 64
HG_SUB = 16
HG_UNROLL = 8
N_EXPERTS = 32
TOP_K = 4
D_FF = D_MODEL
SWIGLU_LIMIT = 7.0
SWIGLU_ALPHA = 1.702
NORM_EPS = 1e-5
NEG_INF = -1e30
LOG2E = 1.4426950408889634
PAGE = 128

VMEM_LIMIT = 56 * 1024 * 1024


def _cp(sem, vmem=VMEM_LIMIT):
    return pltpu.CompilerParams(dimension_semantics=sem, vmem_limit_bytes=vmem)


def _dot(a, b):
    return jnp.dot(a, b, preferred_element_type=F32)


def _dot_nt(a, b):
    return lax.dot_general(a, b, (((1,), (1,)), ((), ())), preferred_element_type=F32)


def _dot_tn(a, b):
    return lax.dot_general(a, b, (((0,), (0,)), ((), ())), preferred_element_type=F32)


def _rms(x, g):
    return x * lax.rsqrt(jnp.mean(x * x, axis=-1, keepdims=True) + NORM_EPS) * g


def _norm_matmul_kernel(x_ref, g_ref, w_ref, o_ref, xn_ref):
    @pl.when(pl.program_id(1) == 0)
    def _():
        xn_ref[...] = _rms(x_ref[...], g_ref[...]).astype(BF16)

    o_ref[...] = _dot(xn_ref[...], w_ref[...])


def _norm_matmul(x, g, w_bf16, tm, tn):
    n, d = x.shape
    m = w_bf16.shape[1]
    return pl.pallas_call(
        _norm_matmul_kernel,
        grid=(n // tm, m // tn),
        in_specs=[
            pl.BlockSpec((tm, d), lambda i, j: (i, 0)),
            pl.BlockSpec((1, d), lambda i, j: (0, 0)),
            pl.BlockSpec((d, tn), lambda i, j: (0, j)),
        ],
        out_specs=pl.BlockSpec((tm, tn), lambda i, j: (i, j)),
        out_shape=jax.ShapeDtypeStruct((n, m), F32),
        scratch_shapes=[pltpu.VMEM((tm, d), BF16)],
        compiler_params=_cp(("parallel", "arbitrary")),
        name="norm_inproj",
    )(x, g.reshape(1, d), w_bf16)


ATT_HEADS_PER_STEP = 4


def _attn_prompt_kernel(lam_ref, q_ref, k_ref, v_ref, km_ref, vm_ref, g_ref, o_ref, kb_ref, vt_ref, kmb_ref, vmt_ref,
                        acc_ref, s_ref, *, tq, scale):
    qi = pl.program_id(2)
    heads = range(ATT_HEADS_PER_STEP)
    lanes = lambda hh: slice(hh * HD, (hh + 1) * HD)

    @pl.when(qi == 0)
    def _():
        for hh in heads:
            kb_ref[hh] = k_ref[0, :, lanes(hh)].astype(BF16)
            vt_ref[hh] = v_ref[0, :, lanes(hh)].T.astype(BF16)
            kmb_ref[hh] = km_ref[:, lanes(hh)].astype(BF16)
            vmt_ref[hh] = vm_ref[:, lanes(hh)].T.astype(BF16)

    def split_q(hh):
        qt = (q_ref[0, :, lanes(hh)] * (ATT_SCALE * LOG2E)).T
        d_idx = lax.broadcasted_iota(jnp.int32, qt.shape, 0)
        return jnp.concatenate([jnp.where(d_idx < DK_ATT, qt, 0.0), jnp.where(d_idx >= DK_ATT, qt, 0.0)],
                               axis=1).astype(BF16)

    q2t = [split_q(hh) for hh in heads]

    def update(hh, s, m, l, vt_blk):
        m_new = jnp.maximum(m, jnp.max(s, axis=0, keepdims=True))
        alpha = jnp.exp2(m - m_new)
        p = jnp.exp2(s - m_new)
        l = alpha * l + jnp.sum(p, axis=0, keepdims=True)
        acc_ref[hh] = acc_ref[hh] * alpha + _dot(vt_blk, p.astype(BF16))
        return m_new, l

    def scores(hh, kj):
        off = pl.multiple_of(kj * tq, tq)
        return _dot(kb_ref[hh, pl.ds(off, tq), :], q2t[hh])

    carry = []
    for hh in heads:
        s = _dot(kmb_ref[hh], q2t[hh])
        m = jnp.max(s, axis=0, keepdims=True)
        p = jnp.exp2(s - m)
        carry += [m, jnp.sum(p, axis=0, keepdims=True)]
        acc_ref[hh] = _dot(vmt_ref[hh], p.astype(BF16))
        s_ref[hh] = scores(hh, 0)

    def body(kj, carry):
        off = pl.multiple_of(kj * tq, tq)
        out = []
        for hh in heads:
            s_cur = s_ref[hh]
            s_ref[hh] = scores(hh, kj + 1)
            out += update(hh, s_cur, carry[2 * hh], carry[2 * hh + 1], vt_ref[hh, :, pl.ds(off, tq)])
        return tuple(out)

    carry = lax.fori_loop(0, qi, body, tuple(carry))

    off = pl.multiple_of(qi * tq, tq)
    key = lax.broadcasted_iota(jnp.int32, (tq, tq), 0)
    qry = lax.broadcasted_iota(jnp.int32, (tq, tq), 1)
    causal = key <= qry
    causal2 = jnp.concatenate([causal, causal], axis=1)
    for hh in heads:
        s = jnp.where(causal2, s_ref[hh], NEG_INF)
        _, l = update(hh, s, carry[2 * hh], carry[2 * hh + 1], vt_ref[hh, :, pl.ds(off, tq)])
        o = acc_ref[hh] / l
        att = (o[:, :tq] - lam_ref[0] * o[:, tq:]).T
        o_ref[0, :, lanes(hh)] = (_rms(att, g_ref[...]) * scale).astype(o_ref.dtype)


def _attn_prompt(lam, proj3, projm, g, scale, tq=256):
    b, t, _ = proj3.shape
    hp = ATT_HEADS_PER_STEP
    w = hp * HD
    nh = H // hp
    kern = functools.partial(_attn_prompt_kernel, tq=tq, scale=scale)
    return pl.pallas_call(
        kern,
        grid=(b, nh, t // tq),
        in_specs=[
            pl.BlockSpec(memory_space=pltpu.SMEM),
            pl.BlockSpec((1, tq, w), lambda bi, h, qi: (bi, qi, h)),
            pl.BlockSpec((1, t, w), lambda bi, h, qi: (bi, 0, nh + h)),
            pl.BlockSpec((1, t, w), lambda bi, h, qi: (bi, 0, 2 * nh + h)),
            pl.BlockSpec((N_META, w), lambda bi, h, qi: (0, nh + h)),
            pl.BlockSpec((N_META, w), lambda bi, h, qi: (0, 2 * nh + h)),
            pl.BlockSpec((1, HD), lambda bi, h, qi: (0, 0)),
        ],
        out_specs=pl.BlockSpec((1, tq, w), lambda bi, h, qi: (bi, qi, h)),
        out_shape=jax.ShapeDtypeStruct((b, t, SEG), BF16),
        scratch_shapes=[pltpu.VMEM((hp, t, HD), BF16), pltpu.VMEM((hp, HD, t), BF16),
                        pltpu.VMEM((hp, N_META, HD), BF16), pltpu.VMEM((hp, HD, N_META), BF16),
                        pltpu.VMEM((hp, HD, 2 * tq), F32), pltpu.VMEM((hp, tq, 2 * tq), F32)],
        compiler_params=_cp(("parallel", "parallel", "arbitrary")),
        name="attn_prompt",
    )(lam, proj3, proj3, proj3, projm, projm, g.reshape(1, HD))


DEC_PAGES = 8


def _split_q(q):
    lane = lax.broadcasted_iota(jnp.int32, q.shape, 1)
    q = q * (ATT_SCALE * LOG2E)
    return jnp.concatenate([jnp.where(lane < DK_ATT, q, 0.0), jnp.where(lane >= DK_ATT, q, 0.0)], axis=0).astype(BF16)


def _attn_decode_kernel(pt_ref, lam_ref, q_ref, kn_ref, vn_ref, ck_ref, cv_ref, g_ref, o_ref,
                        kbuf_ref, vbuf_ref, sem, m_ref, l_ref, acc_ref, bias_ref, *, n_tok, n_pages, scale):
    bi, gi = pl.program_id(0), pl.program_id(1)
    n_groups = pl.num_programs(1)
    step = bi * n_groups + gi
    slot = step % 2
    rows = 2 * n_tok * H

    def page_copies(st, sl):
        out = []
        for j in range(DEC_PAGES):
            page = pt_ref[st * DEC_PAGES + j]
            out.append(pltpu.make_async_copy(ck_ref.at[page], kbuf_ref.at[sl, j], sem.at[0, sl]))
            out.append(pltpu.make_async_copy(cv_ref.at[page], vbuf_ref.at[sl, j], sem.at[1, sl]))
        return out

    @pl.when(step == 0)
    def _():
        for c in page_copies(step, slot):
            c.start()
        r = lax.broadcasted_iota(jnp.int32, bias_ref.shape, 0)
        c = lax.broadcasted_iota(jnp.int32, bias_ref.shape, 1)
        bias_ref[...] = jnp.where((r % H) == (c % H), 0.0, NEG_INF).astype(F32)

    @pl.when(step + 1 < pl.num_programs(0) * n_groups)
    def _():
        for c in page_copies(step + 1, 1 - slot):
            c.start()

    @pl.when(gi == 0)
    def _():
        m_ref[...] = jnp.full(m_ref.shape, NEG_INF, F32)
        l_ref[...] = jnp.zeros(l_ref.shape, F32)
        acc_ref[...] = jnp.zeros(acc_ref.shape, F32)

    q2 = _split_q(q_ref[0])

    def step_update(s, vblk):
        m = m_ref[...]
        m_new = jnp.maximum(m, jnp.max(s, axis=-1, keepdims=True))
        alpha = jnp.exp2(m - m_new)
        p = jnp.exp2(s - m_new)
        l_ref[...] = alpha * l_ref[...] + jnp.sum(p, axis=-1, keepdims=True)
        acc_ref[...] = alpha * acc_ref[...] + _dot(p.astype(BF16), vblk)
        m_ref[...] = m_new

    for c in page_copies(step, slot):
        c.wait()
    nkeys = DEC_PAGES * PAGE * H
    kp = kbuf_ref[slot].reshape(nkeys, HD).astype(BF16)
    vp = vbuf_ref[slot].reshape(nkeys, HD).astype(BF16)
    step_update(_dot_nt(q2, kp) + bias_ref[...], vp)

    @pl.when(gi == n_groups - 1)
    def _():
        nk = n_tok * H
        s = _dot_nt(q2, kn_ref[0].astype(BF16)) + bias_ref[:, :nk]
        r = lax.broadcasted_iota(jnp.int32, (rows, nk), 0)
        c = lax.broadcasted_iota(jnp.int32, (rows, nk), 1)
        s = jnp.where((c // H) <= ((r // H) % n_tok), s, NEG_INF)
        step_update(s, vn_ref[0].astype(BF16))
        o = acc_ref[...] / l_ref[...]
        half = rows // 2
        att = o[:half] - lam_ref[0] * o[half:]
        o_ref[0] = (_rms(att, g_ref[...]) * scale).astype(o_ref.dtype)


def _attn_decode(lam, page_table, q, kn, vn, cache_k, cache_v, g, scale):
    b, th, _ = q.shape
    n_tok = th // H
    n_pages = page_table.shape[1]
    rows = 2 * th
    kern = functools.partial(_attn_decode_kernel, n_tok=n_tok, n_pages=n_pages, scale=scale)
    grid_spec = pltpu.PrefetchScalarGridSpec(
        num_scalar_prefetch=1,
        grid=(b, n_pages // DEC_PAGES),
        in_specs=[
            pl.BlockSpec(memory_space=pltpu.SMEM),
            pl.BlockSpec((1, th, HD), lambda bi, p, pt: (bi, 0, 0)),
            pl.BlockSpec((1, th, HD), lambda bi, p, pt: (bi, 0, 0)),
            pl.BlockSpec((1, th, HD), lambda bi, p, pt: (bi, 0, 0)),
            pl.BlockSpec(memory_space=pl.ANY),
            pl.BlockSpec(memory_space=pl.ANY),
            pl.BlockSpec((1, HD), lambda bi, p, pt: (0, 0)),
        ],
        out_specs=pl.BlockSpec((1, th, HD), lambda bi, p, pt: (bi, 0, 0)),
        scratch_shapes=[pltpu.VMEM((2, DEC_PAGES, PAGE, H, HD), F32), pltpu.VMEM((2, DEC_PAGES, PAGE, H, HD), F32),
                        pltpu.SemaphoreType.DMA((2, 2)),
                        pltpu.VMEM((rows, 1), F32), pltpu.VMEM((rows, 1), F32), pltpu.VMEM((rows, HD), F32),
                        pltpu.VMEM((rows, DEC_PAGES * PAGE * H), F32)],
    )
    return pl.pallas_call(
        kern,
        grid_spec=grid_spec,
        out_shape=jax.ShapeDtypeStruct((b, th, HD), BF16),
        compiler_params=_cp(("arbitrary", "arbitrary")),
        name="attn_decode",
    )(page_table.reshape(-1), lam, q, kn, vn, cache_k, cache_v, g.reshape(1, HD))


def _hgrn_intra(q, kk, iv, b, sub):
    c = q.shape[0]
    t_idx = lax.broadcasted_iota(jnp.int32, (sub, HD), 0)
    s_idx = lax.broadcasted_iota(jnp.int32, (sub, HD), 1)
    outs = []
    for r0 in range(0, c, sub):
        bi, qi, ki, vi = b[r0:r0 + sub], q[r0:r0 + sub], kk[r0:r0 + sub], iv[r0:r0 + sub]
        a = jnp.zeros((sub, HD), F32)
        for s in range(sub):
            e = jnp.exp2(bi - bi[s:s + 1])
            a = jnp.where(s_idx == s, jnp.sum(qi * ki[s:s + 1] * e, axis=-1, keepdims=True), a)
        a = jnp.where(s_idx <= t_idx, a, 0.0)
        od = _dot(a[:, :sub].astype(BF16), vi.astype(BF16))
        if r0 > 0:
            ref = b[r0 - 1:r0]
            qs = (qi * jnp.exp2(bi - ref)).astype(BF16)
            ks = (kk[:r0] * jnp.exp2(ref - b[:r0])).astype(BF16)
            od = od + _dot(_dot_nt(qs, ks).astype(BF16), iv[:r0].astype(BF16))
        outs.append(od)
    return jnp.concatenate(outs, axis=0) if len(outs) > 1 else outs[0]


def _hgrn_kernel(q_ref, f_ref, i_ref, gt_ref, s0_ref, lb_ref, g_ref, o_ref, s_ref, st_ref, b_ref, kk_ref, oi_ref,
                 *, chunk, sub):
    ti = pl.program_id(2)
    nb, tb = q_ref.shape[0], q_ref.shape[1]
    n_chunks = tb // chunk
    lb = lb_ref[0]
    g = g_ref[...]

    for bb in range(nb):
        @pl.when(ti == 0)
        def _():
            st_ref[bb] = s0_ref[bb if s0_ref.shape[0] == nb else 0].T

        f = lb + (1.0 - lb) * jax.nn.sigmoid(f_ref[bb])
        kk_ref[bb] = 1.0 - f
        b = jnp.log(f) * LOG2E
        rc = lax.broadcasted_iota(jnp.int32, b.shape, 0) % chunk
        sh = 1
        while sh < chunk:
            b = b + jnp.where(rc >= sh, pltpu.roll(b, sh, axis=0), 0.0)
            sh *= 2
        b_ref[bb] = b

        def intra(ci, carry):
            sl = pl.ds(pl.multiple_of(ci * chunk, chunk), chunk)
            oi_ref[bb, sl, :] = _hgrn_intra(q_ref[bb, sl, :], kk_ref[bb, sl, :], i_ref[bb, sl, :], b_ref[bb, sl, :],
                                            sub)
            return carry

        lax.fori_loop(0, n_chunks, intra, 0, unroll=min(n_chunks, HG_UNROLL))

        for ci in range(n_chunks):
            sl = slice(ci * chunk, (ci + 1) * chunk)
            bc = b_ref[bb, sl, :]
            bl = bc[chunk - 1:chunk]
            iv = i_ref[bb, sl, :].astype(BF16)
            st = st_ref[bb]
            o = oi_ref[bb, sl, :] + _dot_nt((q_ref[bb, sl, :] * jnp.exp2(bc)).astype(BF16), st.astype(BF16))
            khat = (kk_ref[bb, sl, :] * jnp.exp2(bl - bc)).astype(BF16)
            st_ref[bb] = st * jnp.exp2(bl) + _dot_tn(iv, khat)
            gt = gt_ref[bb, sl, :]
            o_ref[bb, sl, :] = (_rms(o, g) * (gt * jax.nn.sigmoid(gt))).astype(o_ref.dtype)

        @pl.when(ti == pl.num_programs(2) - 1)
        def _():
            s_ref[bb] = st_ref[bb].T


def _hgrn(proj3, s0, lb, g, chunk, tb, nb=1):
    b, t, _ = proj3.shape
    sub = min(chunk, HG_SUB)
    shared_s0 = s0.shape[0] != b
    kern = functools.partial(_hgrn_kernel, chunk=chunk, sub=sub)
    seg = lambda k: pl.BlockSpec((nb, tb, HD), lambda bi, h, ti: (bi, ti, k * H + h))
    return pl.pallas_call(
        kern,
        grid=(b // nb, H, t // tb),
        in_specs=[
            seg(3), seg(4), seg(5), seg(6),
            pl.BlockSpec((1 if shared_s0 else nb, None, HD, HD), lambda bi, h, ti: (0 if shared_s0 else bi, h, 0, 0)),
            pl.BlockSpec((1, 1, HD), lambda bi, h, ti: (h, 0, 0)),
            pl.BlockSpec((1, HD), lambda bi, h, ti: (0, 0)),
        ],
        out_specs=[
            pl.BlockSpec((nb, tb, HD), lambda bi, h, ti: (bi, ti, h)),
            pl.BlockSpec((nb, None, HD, HD), lambda bi, h, ti: (bi, h, 0, 0)),
        ],
        out_shape=[jax.ShapeDtypeStruct((b, t, SEG), BF16), jax.ShapeDtypeStruct((b, H, HD, HD), F32)],
        scratch_shapes=[pltpu.VMEM((nb, HD, HD), F32), pltpu.VMEM((nb, tb, HD), F32), pltpu.VMEM((nb, tb, HD), F32),
                        pltpu.VMEM((nb, tb, HD), F32)],
        compiler_params=_cp(("parallel", "parallel", "arbitrary")),
        name="hgrn",
    )(proj3, proj3, proj3, proj3, s0, lb, g.reshape(1, HD))


def _outproj_router_kernel(att_ref, hg_ref, x_ref, wo_ref, g_ref, wr_ref, br_ref, cin_ref,
                           h_ref, xn_ref, ti_ref, gate_ref, rank_ref, cout_ref, tri_ref, cnt_ref, *, tm):
    i = pl.program_id(0)

    @pl.when(i == 0)
    def _():
        r = lax.broadcasted_iota(jnp.int32, (tm, tm), 0)
        c = lax.broadcasted_iota(jnp.int32, (tm, tm), 1)
        tri_ref[...] = jnp.where(c < r, 1.0, 0.0).astype(BF16)
        cnt_ref[...] = cin_ref[...]

    half = att_ref.shape[1]
    mix = _dot(att_ref[...], wo_ref[:half, :]) + _dot(hg_ref[...], wo_ref[half:, :])
    h = x_ref[...] + mix
    h_ref[...] = h
    xn = _rms(h, g_ref[...])
    xn_ref[...] = xn
    xn_hi = xn.astype(BF16)
    xn_lo = (xn - xn_hi.astype(F32)).astype(BF16)
    t = _dot(xn_hi, wr_ref[...])
    logits = (t[:, :N_EXPERTS] + t[:, N_EXPERTS:] + _dot(xn_lo, wr_ref[:, :N_EXPERTS])) + br_ref[...]

    lane = lax.broadcasted_iota(jnp.int32, logits.shape, 1).astype(F32)
    k_lane = lax.broadcasted_iota(jnp.int32, (tm, TOP_K), 1)
    work = logits
    sel = jnp.zeros(logits.shape, F32)
    top_v, top_i = [], []
    for _ in range(TOP_K):
        mx = jnp.max(work, axis=-1, keepdims=True)
        idx = jnp.min(jnp.where(work == mx, lane, float(N_EXPERTS)), axis=-1, keepdims=True)
        hit = lane == idx
        sel = jnp.where(hit, 1.0, sel)
        work = jnp.where(hit, -jnp.inf, work)
        top_v.append(mx)
        top_i.append(idx)
    ex = [jnp.exp(v - top_v[0]) for v in top_v]
    den = ex[0] + ex[1] + ex[2] + ex[3]
    before = _dot(tri_ref[...], sel.astype(BF16)) + cnt_ref[...]
    ti_out = jnp.zeros((tm, TOP_K), jnp.int32)
    gate_out = jnp.zeros((tm, TOP_K), F32)
    rank_out = jnp.zeros((tm, TOP_K), jnp.int32)
    for k in range(TOP_K):
        rk = jnp.sum(jnp.where(lane == top_i[k], before, 0.0), axis=-1, keepdims=True).astype(jnp.int32)
        ti_out = jnp.where(k_lane == k, top_i[k].astype(jnp.int32), ti_out)
        gate_out = jnp.where(k_lane == k, ex[k] / den, gate_out)
        rank_out = jnp.where(k_lane == k, rk, rank_out)
    ti_ref[...] = ti_out
    gate_ref[...] = gate_out
    rank_ref[...] = rank_out
    cnt_ref[...] = cnt_ref[...] + jnp.sum(sel, axis=0, keepdims=True)
    cout_ref[...] = cnt_ref[...]


def _outproj_router(att, hg, x, wo_bf16, g, wr2, br, cnt_in, tm):
    n, d = x.shape
    half = att.shape[1]
    kern = functools.partial(_outproj_router_kernel, tm=tm)
    row = lambda w: pl.BlockSpec((tm, w), lambda i: (i, 0))
    full = lambda a, b: pl.BlockSpec((a, b), lambda i: (0, 0))
    return pl.pallas_call(
        kern,
        grid=(n // tm,),
        in_specs=[row(half), row(half), row(d), full(2 * half, d), full(1, d), full(d, 2 * N_EXPERTS),
                  full(1, N_EXPERTS), full(1, N_EXPERTS)],
        out_specs=[row(d), row(d), row(TOP_K), row(TOP_K), row(TOP_K), full(1, N_EXPERTS)],
        out_shape=[jax.ShapeDtypeStruct((n, d), F32), jax.ShapeDtypeStruct((n, d), F32),
                   jax.ShapeDtypeStruct((n, TOP_K), jnp.int32), jax.ShapeDtypeStruct((n, TOP_K), F32),
                   jax.ShapeDtypeStruct((n, TOP_K), jnp.int32), jax.ShapeDtypeStruct((1, N_EXPERTS), F32)],
        scratch_shapes=[pltpu.VMEM((tm, tm), BF16), pltpu.VMEM((1, N_EXPERTS), F32)],
        compiler_params=_cp(("arbitrary",)),
        name="outproj_router",
    )(att, hg, x, wo_bf16, g.reshape(1, d), wr2, br.reshape(1, N_EXPERTS), cnt_in)


def _row_copy(src_ref, dst_ref, s, d, sem):
    return pltpu.make_async_copy(src_ref.at[pl.ds(s, 1), :], dst_ref.at[pl.ds(d, 1), :], sem)


SLOT_ALIGN = 8


def _dispatch_kernel(pos_ref, pad0_ref, padn_ref, xa_ref, xb_ref, xs_ref, sem, zero_ref, zsem, *, tm, n_a_tiles):
    i = pl.program_id(0)
    n_tail = zero_ref.shape[0]
    tail_copy = pltpu.make_async_copy(zero_ref, xs_ref.at[pl.ds(xs_ref.shape[0] - n_tail, n_tail), :], zsem)

    def pad_copy(e, j):
        return _row_copy(zero_ref, xs_ref, 0, pad0_ref[e] + j, zsem)

    def pad_rows(fn):
        def body(e, c):
            for j in range(SLOT_ALIGN - 1):
                @pl.when(j < padn_ref[e])
                def _():
                    fn(pad_copy(e, j))
            return c
        lax.fori_loop(0, N_EXPERTS, body, 0)

    @pl.when(i == 0)
    def _():
        zero_ref[...] = jnp.zeros(zero_ref.shape, zero_ref.dtype)
        tail_copy.start()
        tail_copy.wait()
        pad_rows(lambda c: c.start())
        pad_rows(lambda c: c.wait())

    def move(x_ref):
        def start(r, c):
            for k in range(TOP_K):
                _row_copy(x_ref, xs_ref, r, pos_ref[r * TOP_K + k], sem).start(priority=k % 2)
            return c

        lax.fori_loop(0, tm, start, 0)

        def wait(r, c):
            for k in range(TOP_K):
                _row_copy(x_ref, xs_ref, r, pos_ref[r * TOP_K + k], sem).wait()
            return c

        lax.fori_loop(0, tm, wait, 0)

    @pl.when(i < n_a_tiles)
    def _():
        move(xa_ref)

    @pl.when(i >= n_a_tiles)
    def _():
        move(xb_ref)


def _dispatch(pos, xa, xb, tm, n_rows, n_tail, pad0, padn):
    na, d = xa.shape
    nb = xb.shape[0]
    n_a_tiles = na // tm
    kern = functools.partial(_dispatch_kernel, tm=tm, n_a_tiles=n_a_tiles)
    return pl.pallas_call(
        kern,
        grid=((na + nb) // tm,),
        in_specs=[
            pl.BlockSpec((tm * TOP_K,), lambda i: (i,), memory_space=pltpu.SMEM),
            pl.BlockSpec(memory_space=pltpu.SMEM),
            pl.BlockSpec(memory_space=pltpu.SMEM),
            pl.BlockSpec((tm, d), lambda i: (jnp.minimum(i, n_a_tiles - 1), 0)),
            pl.BlockSpec((tm, d), lambda i: (jnp.maximum(i - n_a_tiles, 0), 0)),
        ],
        out_specs=pl.BlockSpec(memory_space=pl.ANY),
        out_shape=jax.ShapeDtypeStruct((n_rows, d), xa.dtype),
        scratch_shapes=[pltpu.SemaphoreType.DMA, pltpu.VMEM((n_tail, d), xa.dtype), pltpu.SemaphoreType.DMA],
        compiler_params=_cp(("arbitrary",)),
        name="dispatch",
    )(pos, pad0, padn, xa, xb)


EXPERT_T = 2304
EXPERT_SUB = 768
EXPERT_STG = 384
EXPERT_TF = 256


def _expert_kernel(te_ref, row0_ref, ns_ref, xs_ref, wg_ref, wu_ref, bg_ref, bu_ref, wd_ref, bd_ref, ys_ref,
                   xb_ref, act_ref, stg_ref, ybuf_ref, wgb_ref, wub_ref, wdb_ref, sem_in, sem_out, *, n_defined):
    del te_ref
    i, s = pl.program_id(0), pl.program_id(1)
    nf = act_ref.shape[0]
    tf = act_ref.shape[2]
    sub, stg = EXPERT_SUB, EXPERT_STG
    n_sub = EXPERT_T // sub
    per_sub = sub // stg
    ns, r0 = ns_ref[i], row0_ref[i]
    in_a = s < nf
    in_b = jnp.logical_not(in_a)
    slot = s % 2
    n_pieces = n_sub * per_sub
    assert n_pieces + 1 <= nf
    i_next = jnp.minimum(i + 1, pl.num_programs(0) - 1)
    pieces_next = jnp.where(i + 1 < pl.num_programs(0), ns_ref[i_next], 0) * per_sub
    r0_next = row0_ref[i_next]

    def x_copy(p, item_r0=None):
        item_r0 = r0 if item_r0 is None else item_r0
        src = xs_ref.at[pl.ds(pl.multiple_of(item_r0 + p * stg, SLOT_ALIGN), stg), :]
        return pltpu.make_async_copy(src, stg_ref.at[p % 2], sem_in.at[p % 2])

    def y_copy(j, sl):
        col = pl.ds(pl.multiple_of((s - nf) * tf, tf), tf)
        dst = ys_ref.at[pl.ds(pl.multiple_of(r0 + j * sub, SLOT_ALIGN), sub), col]
        return pltpu.make_async_copy(ybuf_ref.at[sl, j], dst, sem_out.at[sl, j])

    @pl.when((i == 0) & (s == 0))
    def _():
        n_tail = ys_ref.shape[0] - n_defined
        chunk = min(n_tail, stg)
        stg_ref[0, 0:chunk, :] = jnp.zeros((chunk, stg_ref.shape[2]), F32)
        for c0 in range(0, n_tail, chunk):
            c0 = min(c0, n_tail - chunk)
            tail = pltpu.make_async_copy(stg_ref.at[0, pl.ds(0, chunk), :],
                                         ys_ref.at[pl.ds(n_defined + c0, chunk), :], sem_in.at[0])
            tail.start()
            tail.wait()

    first_load = (i == 0) & (s == 0)

    @pl.when(first_load & (ns > 0))
    def _():
        x_copy(0).start()
        x_copy(1).start()

    @pl.when(in_a & (ns > 0))
    def _():
        wgb_ref[...] = wg_ref[...].astype(BF16)
        wub_ref[...] = wu_ref[...].astype(BF16)

    @pl.when(in_b & (ns > 0))
    def _():
        wdb_ref[...] = wd_ref[...].astype(BF16)

    for p in range(n_pieces):
        @pl.when(in_b & (s - nf == p + 1) & (p < pieces_next))
        def _():
            x_copy(p, r0_next).wait()
            xb_ref[p * stg:(p + 1) * stg, :] = stg_ref[p % 2].astype(BF16)

        @pl.when(in_b & (s - nf == p) & (p < pieces_next))
        def _():
            x_copy(p, r0_next).start()

    for j in range(n_sub):
        rows = slice(j * sub, (j + 1) * sub)

        @pl.when(in_a & (j < ns))
        def _():
            @pl.when(first_load)
            def _():
                for p in range(j * per_sub, (j + 1) * per_sub):
                    x_copy(p).wait()
                    xb_ref[p * stg:(p + 1) * stg, :] = stg_ref[p % 2].astype(BF16)
                    if p + 2 < n_pieces:
                        @pl.when(p + 2 < ns * per_sub)
                        def _():
                            x_copy(p + 2).start()

            x = xb_ref[rows, :]
            gate = jnp.minimum(_dot(x, wgb_ref[...]) + bg_ref[0], SWIGLU_LIMIT)
            up = jnp.clip(_dot(x, wub_ref[...]) + bu_ref[0], -SWIGLU_LIMIT, SWIGLU_LIMIT)
            act = (up + 1.0) * gate * jax.nn.sigmoid(SWIGLU_ALPHA * gate)
            act_ref[jnp.minimum(s, nf - 1), rows, :] = act.astype(BF16)

        @pl.when(in_b & (j < ns))
        def _():
            @pl.when(s >= nf + 2)
            def _():
                y_copy(j, slot).wait()

            a = jnp.concatenate([act_ref[f, rows, :] for f in range(nf)], axis=1)
            ybuf_ref[slot, j] = _dot(a, wdb_ref[...]) + bd_ref[0]
            y_copy(j, slot).start()

            @pl.when(s == 2 * nf - 1)
            def _():
                y_copy(j, 1 - slot).wait()
                y_copy(j, slot).wait()


def _experts(item_expert, item_row0, item_nsub, xs, w_gate_up, b_gate_up, w_down, b_down, n_defined):
    rows, d = xs.shape
    dff = w_down.shape[1]
    tf = EXPERT_TF
    nf = dff // tf
    n_items = item_expert.shape[0]
    fa = lambda s: jnp.minimum(s, nf - 1)
    cb = lambda s: jnp.maximum(s - nf, 0)
    grid_spec = pltpu.PrefetchScalarGridSpec(
        num_scalar_prefetch=3,
        grid=(n_items, 2 * nf),
        in_specs=[
            pl.BlockSpec(memory_space=pl.ANY),
            pl.BlockSpec((None, d, tf), lambda i, s, te, r0, ns: (te[i], 0, fa(s))),
            pl.BlockSpec((None, d, tf), lambda i, s, te, r0, ns: (te[i], 0, nf + fa(s))),
            pl.BlockSpec((None, 1, tf), lambda i, s, te, r0, ns: (te[i], 0, fa(s))),
            pl.BlockSpec((None, 1, tf), lambda i, s, te, r0, ns: (te[i], 0, nf + fa(s))),
            pl.BlockSpec((None, dff, tf), lambda i, s, te, r0, ns: (te[i], 0, cb(s))),
            pl.BlockSpec((None, 1, tf), lambda i, s, te, r0, ns: (te[i], 0, cb(s))),
        ],
        out_specs=pl.BlockSpec(memory_space=pl.ANY),
        scratch_shapes=[pltpu.VMEM((EXPERT_T, d), BF16), pltpu.VMEM((nf, EXPERT_T, tf), BF16),
                        pltpu.VMEM((2, EXPERT_STG, d), F32),
                        pltpu.VMEM((2, EXPERT_T // EXPERT_SUB, EXPERT_SUB, tf), F32),
                        pltpu.VMEM((d, tf), BF16), pltpu.VMEM((d, tf), BF16), pltpu.VMEM((dff, tf), BF16),
                        pltpu.SemaphoreType.DMA((2,)),
                        pltpu.SemaphoreType.DMA((2, EXPERT_T // EXPERT_SUB))],
    )
    return pl.pallas_call(
        functools.partial(_expert_kernel, n_defined=n_defined),
        grid_spec=grid_spec,
        out_shape=jax.ShapeDtypeStruct((rows, d), F32),
        compiler_params=_cp(("arbitrary", "arbitrary")),
        name="experts",
    )(item_expert, item_row0, item_nsub, xs, w_gate_up, w_gate_up, b_gate_up, b_gate_up, w_down, b_down)


def _combine_kernel(pos_ref, posn_ref, h_ref, gate_ref, g_ref, ys_ref, o_ref, buf_ref, sem, *, tm):
    i = pl.program_id(0)
    slot = i % 2

    def gather(p_ref, sl, fn):
        def body(r, c):
            for k in range(TOP_K):
                fn(_row_copy(ys_ref, buf_ref.at[sl, k], p_ref[r * TOP_K + k], r, sem.at[sl]), k)
            return c
        lax.fori_loop(0, tm, body, 0)

    start = lambda c, k: c.start(priority=k % 2)

    @pl.when(i == 0)
    def _():
        gather(pos_ref, slot, start)

    @pl.when(i + 1 < pl.num_programs(0))
    def _():
        gather(posn_ref, 1 - slot, start)

    gather(pos_ref, slot, lambda c, k: c.wait())

    gates = gate_ref[...]
    out = h_ref[...]
    for k in range(TOP_K):
        out = out + gates[:, k:k + 1] * buf_ref[slot, k]
    o_ref[...] = _rms(out, g_ref[...])


def _combine(pos, h, gates, g, ys, tm):
    n, d = h.shape
    n_tiles = n // tm
    kern = functools.partial(_combine_kernel, tm=tm)
    return pl.pallas_call(
        kern,
        grid=(n_tiles,),
        in_specs=[
            pl.BlockSpec((tm * TOP_K,), lambda i: (i,), memory_space=pltpu.SMEM),
            pl.BlockSpec((tm * TOP_K,), lambda i: (jnp.minimum(i + 1, n_tiles - 1),), memory_space=pltpu.SMEM),
            pl.BlockSpec((tm, d), lambda i: (i, 0)),
            pl.BlockSpec((tm, TOP_K), lambda i: (i, 0)),
            pl.BlockSpec((1, d), lambda i: (0, 0)),
            pl.BlockSpec(memory_space=pl.ANY),
        ],
        out_specs=pl.BlockSpec((tm, d), lambda i: (i, 0)),
        out_shape=jax.ShapeDtypeStruct((n, d), F32),
        scratch_shapes=[pltpu.VMEM((2, TOP_K, tm, d), F32), pltpu.SemaphoreType.DMA((2,))],
        compiler_params=_cp(("arbitrary",)),
        name="combine",
    )(pos, pos, h, gates, g.reshape(1, d), ys)


TOKEN_TM = 256


def kernel(x_prompt, x_sample, cache_k, cache_v, state_hgrn, page_table, meta_tokens, norm_mix_g, w_in, lambda_q1, lambda_k1, lambda_q2, lambda_k2, attn_sub_g, hgrn_gamma, hgrn_norm_g, w_out, norm_ffn_g, w_router, b_router, w_gate_up, b_gate_up, w_down, b_down, norm_final_g):
    l = 0
    bp, tp, d = x_prompt.shape
    bs, ts, _ = x_sample.shape
    n_p, n_s = bp * tp, bs * ts

    lam_init = 0.8 - 0.6 * math.exp(-0.3 * l)
    lam = (jnp.exp(jnp.sum(lambda_q1[l].astype(F32) * lambda_k1[l].astype(F32)))
           - jnp.exp(jnp.sum(lambda_q2[l].astype(F32) * lambda_k2[l].astype(F32))) + lam_init).reshape(1)
    sub_scale = 1.0 - lam_init
    lb = jnp.cumsum(jax.nn.softmax(hgrn_gamma.astype(F32), axis=0), axis=0)[l].reshape(H, 1, HD)
    w_in_b = w_in[l].astype(BF16)
    w_out_b = w_out[l].astype(BF16)
    wr_hi = w_router[l].astype(BF16)
    wr_lo = (w_router[l] - wr_hi.astype(F32)).astype(BF16)
    wr2 = jnp.concatenate([wr_hi, wr_lo], axis=1)

    proj_p = _norm_matmul(x_prompt.reshape(n_p, d), norm_mix_g[l], w_in_b, 1024, SEG)
    extra = jnp.concatenate([meta_tokens.astype(F32), x_sample.reshape(n_s, d)], axis=0)
    proj_e = _norm_matmul(extra, norm_mix_g[l], w_in_b, extra.shape[0], SEG)
    proj_m, proj_s = proj_e[:N_META], proj_e[N_META:]
    proj_p3 = proj_p.reshape(bp, tp, N_SEG * SEG)

    att_p = _attn_prompt(lam, proj_p3, proj_m, attn_sub_g[l], sub_scale)
    zero_state = jnp.zeros((1, H, HD, HD), F32)
    _, s_meta = _hgrn(proj_m.reshape(1, N_META, N_SEG * SEG), zero_state, lb, hgrn_norm_g[l], N_META, N_META)
    hg_p, s_prompt = _hgrn(proj_p3, s_meta, lb, hgrn_norm_g[l], HG_CHUNK, 512)

    rows_s = lambda k: proj_s[:, k * SEG:(k + 1) * SEG].reshape(bs, ts * H, HD)
    att_s = _attn_decode(lam, page_table, rows_s(0), rows_s(1), rows_s(2), cache_k[l], cache_v[l],
                         attn_sub_g[l], sub_scale)
    hg_s, s_sample = _hgrn(proj_s.reshape(bs, ts, N_SEG * SEG), state_hgrn[l].astype(F32), lb, hgrn_norm_g[l], ts, ts,
                           nb=8)

    cnt0 = jnp.zeros((1, N_EXPERTS), F32)
    h_p, xn_p, ti_p, gate_p, rank_p, cnt1 = _outproj_router(
        att_p.reshape(n_p, SEG), hg_p.reshape(n_p, SEG), x_prompt.reshape(n_p, d), w_out_b, norm_ffn_g[l],
        wr2, b_router[l], cnt0, TOKEN_TM)
    h_s, xn_s, ti_s, gate_s, rank_s, cnt2 = _outproj_router(
        att_s.reshape(n_s, SEG), hg_s.reshape(n_s, SEG), x_sample.reshape(n_s, d), w_out_b, norm_ffn_g[l],
        wr2, b_router[l], cnt1, TOKEN_TM)

    n_slots = (n_p + n_s) * TOP_K + N_EXPERTS * (SLOT_ALIGN - 1)
    n_items = n_slots // EXPERT_T + N_EXPERTS
    counts = cnt2[0].astype(jnp.int32)
    counts_al = (counts + SLOT_ALIGN - 1) // SLOT_ALIGN * SLOT_ALIGN
    offsets = jnp.cumsum(counts_al) - counts_al
    items_per = (counts + EXPERT_T - 1) // EXPERT_T
    item_end = jnp.cumsum(items_per)
    item_start = item_end - items_per
    item_ids = jnp.arange(n_items, dtype=jnp.int32)
    item_valid = item_ids < item_end[-1]
    item_expert = jnp.minimum(jnp.sum((item_end[None, :] <= item_ids[:, None]).astype(jnp.int32), axis=1),
                              N_EXPERTS - 1)
    item_k = item_ids - item_start[item_expert]
    item_rows = jnp.where(item_valid, jnp.clip(counts[item_expert] - item_k * EXPERT_T, 0, EXPERT_T), 0)
    item_row0 = jnp.where(item_valid, offsets[item_expert] + item_k * EXPERT_T, 0)
    item_nsub = (item_rows + EXPERT_SUB - 1) // EXPERT_SUB
    last_expert = item_expert[jnp.maximum(item_end[-1] - 1, 0)]
    item_expert = jnp.where(item_valid, item_expert, last_expert)
    pos_p = (offsets[ti_p] + rank_p).reshape(-1)
    pos_s = (offsets[ti_s] + rank_s).reshape(-1)

    n_routed = (n_p + n_s) * TOP_K
    n_rows = n_slots + EXPERT_SUB
    xs = _dispatch(jnp.concatenate([pos_p, pos_s]), xn_p, xn_s, TOKEN_TM, n_rows, n_rows - n_routed,
                   offsets + counts, counts_al - counts)
    ys = _experts(item_expert, item_row0, item_nsub, xs, w_gate_up[l],
                  b_gate_up[l].reshape(N_EXPERTS, 1, 2 * D_FF), w_down[l], b_down[l].reshape(N_EXPERTS, 1, d),
                  n_routed)
    y_p = _combine(pos_p, h_p, gate_p, norm_final_g, ys, TOKEN_TM)
    y_s = _combine(pos_s, h_s, gate_s, norm_final_g, ys, TOKEN_TM)

    def kv_prompt(k):
        own = proj_p3[:, :, k * SEG:(k + 1) * SEG]
        meta = jnp.broadcast_to(proj_m[None, :, k * SEG:(k + 1) * SEG], (bp, N_META, SEG))
        return jnp.concatenate([meta, own], axis=1).reshape(1, bp, N_META + tp, H, HD)

    k_sample = proj_s[:, SEG:2 * SEG].reshape(1, bs, ts, H, HD)
    v_sample = proj_s[:, 2 * SEG:3 * SEG].reshape(1, bs, ts, H, HD)
    return (y_p.reshape(bp, tp, d), y_s.reshape(bs, ts, d), kv_prompt(1), kv_prompt(2), s_prompt[None],
            k_sample, v_sample, s_sample[None].astype(state_hgrn.dtype))
```

```python
import functools
import math

import jax
import jax.numpy as jnp
from jax import lax
from jax.experimental import pallas as pl
from jax.experimental.pallas import tpu as pltpu

F32 = jnp.float32
BF16 = jnp.bfloat16

D_MODEL = 2048
N_META = 16
H = 8
HD = 128
DK_ATT = 64
SEG = H * HD
N_SEG = 7
ATT_SCALE = DK_ATT ** -0.5
HG_CHUNK = 64
HG_SUB = 16
HG_UNROLL = 8
N_EXPERTS = 32
TOP_K = 4
D_FF = D_MODEL
SWIGLU_LIMIT = 7.0
SWIGLU_ALPHA = 1.702
NORM_EPS = 1e-5
NEG_INF = -1e30
LOG2E = 1.4426950408889634
PAGE = 128

VMEM_LIMIT = 56 * 1024 * 1024


def _cp(sem, vmem=VMEM_LIMIT):
    return pltpu.CompilerParams(dimension_semantics=sem, vmem_limit_bytes=vmem)


def _dot(a, b):
    return jnp.dot(a, b, preferred_element_type=F32)


def _dot_nt(a, b):
    return lax.dot_general(a, b, (((1,), (1,)), ((), ())), preferred_element_type=F32)


def _dot_tn(a, b):
    return lax.dot_general(a, b, (((0,), (0,)), ((), ())), preferred_element_type=F32)


def _rms(x, g):
    return x * lax.rsqrt(jnp.mean(x * x, axis=-1, keepdims=True) + NORM_EPS) * g


def _norm_matmul_kernel(x_ref, g_ref, w_ref, o_ref, xn_ref):
    @pl.when(pl.program_id(1) == 0)
    def _():
        xn_ref[...] = _rms(x_ref[...], g_ref[...]).astype(BF16)

    o_ref[...] = _dot(xn_ref[...], w_ref[...])


def _norm_matmul(x, g, w_bf16, tm, tn):
    n, d = x.shape
    m = w_bf16.shape[1]
    return pl.pallas_call(
        _norm_matmul_kernel,
        grid=(n // tm, m // tn),
        in_specs=[
            pl.BlockSpec((tm, d), lambda i, j: (i, 0)),
            pl.BlockSpec((1, d), lambda i, j: (0, 0)),
            pl.BlockSpec((d, tn), lambda i, j: (0, j)),
        ],
        out_specs=pl.BlockSpec((tm, tn), lambda i, j: (i, j)),
        out_shape=jax.ShapeDtypeStruct((n, m), F32),
        scratch_shapes=[pltpu.VMEM((tm, d), BF16)],
        compiler_params=_cp(("parallel", "arbitrary")),
        name="norm_inproj",
    )(x, g.reshape(1, d), w_bf16)


ATT_HEADS_PER_STEP = 4


def _attn_prompt_kernel(lam_ref, q_ref, k_ref, v_ref, km_ref, vm_ref, g_ref, o_ref, kb_ref, vt_ref, kmb_ref, vmt_ref,
                        acc_ref, s_ref, *, tq, scale):
    qi = pl.program_id(2)
    heads = range(ATT_HEADS_PER_STEP)
    lanes = lambda hh: slice(hh * HD, (hh + 1) * HD)

    @pl.when(qi == 0)
    def _():
        for hh in heads:
            kb_ref[hh] = k_ref[0, :, lanes(hh)].astype(BF16)
            vt_ref[hh] = v_ref[0, :, lanes(hh)].T.astype(BF16)
            kmb_ref[hh] = km_ref[:, lanes(hh)].astype(BF16)
            vmt_ref[hh] = vm_ref[:, lanes(hh)].T.astype(BF16)

    def split_q(hh):
        qt = (q_ref[0, :, lanes(hh)] * (ATT_SCALE * LOG2E)).T
        d_idx = lax.broadcasted_iota(jnp.int32, qt.shape, 0)
        return jnp.concatenate([jnp.where(d_idx < DK_ATT, qt, 0.0), jnp.where(d_idx >= DK_ATT, qt, 0.0)],
                               axis=1).astype(BF16)

    q2t = [split_q(hh) for hh in heads]

    def update(hh, s, m, l, vt_blk):
        m_new = jnp.maximum(m, jnp.max(s, axis=0, keepdims=True))
        alpha = jnp.exp2(m - m_new)
        p = jnp.exp2(s - m_new)
        l = alpha * l + jnp.sum(p, axis=0, keepdims=True)
        acc_ref[hh] = acc_ref[hh] * alpha + _dot(vt_blk, p.astype(BF16))
        return m_new, l

    def scores(hh, kj):
        off = pl.multiple_of(kj * tq, tq)
        return _dot(kb_ref[hh, pl.ds(off, tq), :], q2t[hh])

    carry = []
    for hh in heads:
        s = _dot(kmb_ref[hh], q2t[hh])
        m = jnp.max(s, axis=0, keepdims=True)
        p = jnp.exp2(s - m)
        carry += [m, jnp.sum(p, axis=0, keepdims=True)]
        acc_ref[hh] = _dot(vmt_ref[hh], p.astype(BF16))
        s_ref[hh] = scores(hh, 0)

    def body(kj, carry):
        off = pl.multiple_of(kj * tq, tq)
        out = []
        for hh in heads:
            s_cur = s_ref[hh]
            s_ref[hh] = scores(hh, kj + 1)
            out += update(hh, s_cur, carry[2 * hh], carry[2 * hh + 1], vt_ref[hh, :, pl.ds(off, tq)])
        return tuple(out)

    carry = lax.fori_loop(0, qi, body, tuple(carry))

    off = pl.multiple_of(qi * tq, tq)
    key = lax.broadcasted_iota(jnp.int32, (tq, tq), 0)
    qry = lax.broadcasted_iota(jnp.int32, (tq, tq), 1)
    causal = key <= qry
    causal2 = jnp.concatenate([causal, causal], axis=1)
    for hh in heads:
        s = jnp.where(causal2, s_ref[hh], NEG_INF)
        _, l = update(hh, s, carry[2 * hh], carry[2 * hh + 1], vt_ref[hh, :, pl.ds(off, tq)])
        o = acc_ref[hh] / l
        att = (o[:, :tq] - lam_ref[0] * o[:, tq:]).T
        o_ref[0, :, lanes(hh)] = (_rms(att, g_ref[...]) * scale).astype(o_ref.dtype)


def _attn_prompt(lam, proj3, projm, g, scale, tq=256):
    b, t, _ = proj3.shape
    hp = ATT_HEADS_PER_STEP
    w = hp * HD
    nh = H // hp
    kern = functools.partial(_attn_prompt_kernel, tq=tq, scale=scale)
    return pl.pallas_call(
        kern,
        grid=(b, nh, t // tq),
        in_specs=[
            pl.BlockSpec(memory_space=pltpu.SMEM),
            pl.BlockSpec((1, tq, w), lambda bi, h, qi: (bi, qi, h)),
            pl.BlockSpec((1, t, w), lambda bi, h, qi: (bi, 0, nh + h)),
            pl.BlockSpec((1, t, w), lambda bi, h, qi: (bi, 0, 2 * nh + h)),
            pl.BlockSpec((N_META, w), lambda bi, h, qi: (0, nh + h)),
            pl.BlockSpec((N_META, w), lambda bi, h, qi: (0, 2 * nh + h)),
            pl.BlockSpec((1, HD), lambda bi, h, qi: (0, 0)),
        ],
        out_specs=pl.BlockSpec((1, tq, w), lambda bi, h, qi: (bi, qi, h)),
        out_shape=jax.ShapeDtypeStruct((b, t, SEG), BF16),
        scratch_shapes=[pltpu.VMEM((hp, t, HD), BF16), pltpu.VMEM((hp, HD, t), BF16),
                        pltpu.VMEM((hp, N_META, HD), BF16), pltpu.VMEM((hp, HD, N_META), BF16),
                        pltpu.VMEM((hp, HD, 2 * tq), F32), pltpu.VMEM((hp, tq, 2 * tq), F32)],
        compiler_params=_cp(("parallel", "parallel", "arbitrary")),
        name="attn_prompt",
    )(lam, proj3, proj3, proj3, projm, projm, g.reshape(1, HD))


DEC_PAGES = 8


def _split_q(q):
    lane = lax.broadcasted_iota(jnp.int32, q.shape, 1)
    q = q * (ATT_SCALE * LOG2E)
    return jnp.concatenate([jnp.where(lane < DK_ATT, q, 0.0), jnp.where(lane >= DK_ATT, q, 0.0)], axis=0).astype(BF16)


def _attn_decode_kernel(pt_ref, lam_ref, q_ref, kn_ref, vn_ref, ck_ref, cv_ref, g_ref, o_ref,
                        kbuf_ref, vbuf_ref, sem, m_ref, l_ref, acc_ref, bias_ref, *, n_tok, n_pages, scale):
    bi, gi = pl.program_id(0), pl.program_id(1)
    n_groups = pl.num_programs(1)
    step = bi * n_groups + gi
    slot = step % 2
    rows = 2 * n_tok * H

    def page_copies(st, sl):
        out = []
        for j in range(DEC_PAGES):
            page = pt_ref[st * DEC_PAGES + j]
            out.append(pltpu.make_async_copy(ck_ref.at[page], kbuf_ref.at[sl, j], sem.at[0, sl]))
            out.append(pltpu.make_async_copy(cv_ref.at[page], vbuf_ref.at[sl, j], sem.at[1, sl]))
        return out

    @pl.when(step == 0)
    def _():
        for c in page_copies(step, slot):
            c.start()
        r = lax.broadcasted_iota(jnp.int32, bias_ref.shape, 0)
        c = lax.broadcasted_iota(jnp.int32, bias_ref.shape, 1)
        bias_ref[...] = jnp.where((r % H) == (c % H), 0.0, NEG_INF).astype(F32)

    @pl.when(step + 1 < pl.num_programs(0) * n_groups)
    def _():
        for c in page_copies(step + 1, 1 - slot):
            c.start()

    @pl.when(gi == 0)
    def _():
        m_ref[...] = jnp.full(m_ref.shape, NEG_INF, F32)
        l_ref[...] = jnp.zeros(l_ref.shape, F32)
        acc_ref[...] = jnp.zeros(acc_ref.shape, F32)

    q2 = _split_q(q_ref[0])

    def step_update(s, vblk):
        m = m_ref[...]
        m_new = jnp.maximum(m, jnp.max(s, axis=-1, keepdims=True))
        alpha = jnp.exp2(m - m_new)
        p = jnp.exp2(s - m_new)
        l_ref[...] = alpha * l_ref[...] + jnp.sum(p, axis=-1, keepdims=True)
        acc_ref[...] = alpha * acc_ref[...] + _dot(p.astype(BF16), vblk)
        m_ref[...] = m_new

    for c in page_copies(step, slot):
        c.wait()
    nkeys = DEC_PAGES * PAGE * H
    kp = kbuf_ref[slot].reshape(nkeys, HD).astype(BF16)
    vp = vbuf_ref[slot].reshape(nkeys, HD).astype(BF16)
    step_update(_dot_nt(q2, kp) + bias_ref[...], vp)

    @pl.when(gi == n_groups - 1)
    def _():
        nk = n_tok * H
        s = _dot_nt(q2, kn_ref[0].astype(BF16)) + bias_ref[:, :nk]
        r = lax.broadcasted_iota(jnp.int32, (rows, nk), 0)
        c = lax.broadcasted_iota(jnp.int32, (rows, nk), 1)
        s = jnp.where((c // H) <= ((r // H) % n_tok), s, NEG_INF)
        step_update(s, vn_ref[0].astype(BF16))
        o = acc_ref[...] / l_ref[...]
        half = rows // 2
        att = o[:half] - lam_ref[0] * o[half:]
        o_ref[0] = (_rms(att, g_ref[...]) * scale).astype(o_ref.dtype)


def _attn_decode(lam, page_table, q, kn, vn, cache_k, cache_v, g, scale):
    b, th, _ = q.shape
    n_tok = th // H
    n_pages = page_table.shape[1]
    rows = 2 * th
    kern = functools.partial(_attn_decode_kernel, n_tok=n_tok, n_pages=n_pages, scale=scale)
    grid_spec = pltpu.PrefetchScalarGridSpec(
        num_scalar_prefetch=1,
        grid=(b, n_pages // DEC_PAGES),
        in_specs=[
            pl.BlockSpec(memory_space=pltpu.SMEM),
            pl.BlockSpec((1, th, HD), lambda bi, p, pt: (bi, 0, 0)),
            pl.BlockSpec((1, th, HD), lambda bi, p, pt: (bi, 0, 0)),
            pl.BlockSpec((1, th, HD), lambda bi, p, pt: (bi, 0, 0)),
            pl.BlockSpec(memory_space=pl.ANY),
            pl.BlockSpec(memory_space=pl.ANY),
            pl.BlockSpec((1, HD), lambda bi, p, pt: (0, 0)),
        ],
        out_specs=pl.BlockSpec((1, th, HD), lambda bi, p, pt: (bi, 0, 0)),
        scratch_shapes=[pltpu.VMEM((2, DEC_PAGES, PAGE, H, HD), F32), pltpu.VMEM((2, DEC_PAGES, PAGE, H, HD), F32),
                        pltpu.SemaphoreType.DMA((2, 2)),
                        pltpu.VMEM((rows, 1), F32), pltpu.VMEM((rows, 1), F32), pltpu.VMEM((rows, HD), F32),
                        pltpu.VMEM((rows, DEC_PAGES * PAGE * H), F32)],
    )
    return pl.pallas_call(
        kern,
        grid_spec=grid_spec,
        out_shape=jax.ShapeDtypeStruct((b, th, HD), BF16),
        compiler_params=_cp(("arbitrary", "arbitrary")),
        name="attn_decode",
    )(page_table.reshape(-1), lam, q, kn, vn, cache_k, cache_v, g.reshape(1, HD))


def _hgrn_intra(q, kk, iv, b, sub):
    c = q.shape[0]
    t_idx = lax.broadcasted_iota(jnp.int32, (sub, HD), 0)
    s_idx = lax.broadcasted_iota(jnp.int32, (sub, HD), 1)
    outs = []
    for r0 in range(0, c, sub):
        bi, qi, ki, vi = b[r0:r0 + sub], q[r0:r0 + sub], kk[r0:r0 + sub], iv[r0:r0 + sub]
        a = jnp.zeros((sub, HD), F32)
        for s in range(sub):
            e = jnp.exp2(bi - bi[s:s + 1])
            a = jnp.where(s_idx == s, jnp.sum(qi * ki[s:s + 1] * e, axis=-1, keepdims=True), a)
        a = jnp.where(s_idx <= t_idx, a, 0.0)
        od = _dot(a[:, :sub].astype(BF16), vi.astype(BF16))
        if r0 > 0:
            ref = b[r0 - 1:r0]
            qs = (qi * jnp.exp2(bi - ref)).astype(BF16)
            ks = (kk[:r0] * jnp.exp2(ref - b[:r0])).astype(BF16)
            od = od + _dot(_dot_nt(qs, ks).astype(BF16), iv[:r0].astype(BF16))
        outs.append(od)
    return jnp.concatenate(outs, axis=0) if len(outs) > 1 else outs[0]


def _hgrn_kernel(q_ref, f_ref, i_ref, gt_ref, s0_ref, lb_ref, g_ref, o_ref, s_ref, st_ref, b_ref, kk_ref, oi_ref,
                 *, chunk, sub):
    ti = pl.program_id(2)
    nb, tb = q_ref.shape[0], q_ref.shape[1]
    n_chunks = tb // chunk
    lb = lb_ref[0]
    g = g_ref[...]

    for bb in range(nb):
        @pl.when(ti == 0)
        def _():
            st_ref[bb] = s0_ref[bb if s0_ref.shape[0] == nb else 0].T

        f = lb + (1.0 - lb) * jax.nn.sigmoid(f_ref[bb])
        kk_ref[bb] = 1.0 - f
        b = jnp.log(f) * LOG2E
        rc = lax.broadcasted_iota(jnp.int32, b.shape, 0) % chunk
        sh = 1
        while sh < chunk:
            b = b + jnp.where(rc >= sh, pltpu.roll(b, sh, axis=0), 0.0)
            sh *= 2
        b_ref[bb] = b

        def intra(ci, carry):
            sl = pl.ds(pl.multiple_of(ci * chunk, chunk), chunk)
            oi_ref[bb, sl, :] = _hgrn_intra(q_ref[bb, sl, :], kk_ref[bb, sl, :], i_ref[bb, sl, :], b_ref[bb, sl, :],
                                            sub)
            return carry

        lax.fori_loop(0, n_chunks, intra, 0, unroll=min(n_chunks, HG_UNROLL))

        for ci in range(n_chunks):
            sl = slice(ci * chunk, (ci + 1) * chunk)
            bc = b_ref[bb, sl, :]
            bl = bc[chunk - 1:chunk]
            iv = i_ref[bb, sl, :].astype(BF16)
            st = st_ref[bb]
            o = oi_ref[bb, sl, :] + _dot_nt((q_ref[bb, sl, :] * jnp.exp2(bc)).astype(BF16), st.astype(BF16))
            khat = (kk_ref[bb, sl, :] * jnp.exp2(bl - bc)).astype(BF16)
            st_ref[bb] = st * jnp.exp2(bl) + _dot_tn(iv, khat)
            gt = gt_ref[bb, sl, :]
            o_ref[bb, sl, :] = (_rms(o, g) * (gt * jax.nn.sigmoid(gt))).astype(o_ref.dtype)

        @pl.when(ti == pl.num_programs(2) - 1)
        def _():
            s_ref[bb] = st_ref[bb].T


def _hgrn(proj3, s0, lb, g, chunk, tb, nb=1):
    b, t, _ = proj3.shape
    sub = min(chunk, HG_SUB)
    shared_s0 = s0.shape[0] != b
    kern = functools.partial(_hgrn_kernel, chunk=chunk, sub=sub)
    seg = lambda k: pl.BlockSpec((nb, tb, HD), lambda bi, h, ti: (bi, ti, k * H + h))
    return pl.pallas_call(
        kern,
        grid=(b // nb, H, t // tb),
        in_specs=[
            seg(3), seg(4), seg(5), seg(6),
            pl.BlockSpec((1 if shared_s0 else nb, None, HD, HD), lambda bi, h, ti: (0 if shared_s0 else bi, h, 0, 0)),
            pl.BlockSpec((1, 1, HD), lambda bi, h, ti: (h, 0, 0)),
            pl.BlockSpec((1, HD), lambda bi, h, ti: (0, 0)),
        ],
        out_specs=[
            pl.BlockSpec((nb, tb, HD), lambda bi, h, ti: (bi, ti, h)),
            pl.BlockSpec((nb, None, HD, HD), lambda bi, h, ti: (bi, h, 0, 0)),
        ],
        out_shape=[jax.ShapeDtypeStruct((b, t, SEG), BF16), jax.ShapeDtypeStruct((b, H, HD, HD), F32)],
        scratch_shapes=[pltpu.VMEM((nb, HD, HD), F32), pltpu.VMEM((nb, tb, HD), F32), pltpu.VMEM((nb, tb, HD), F32),
                        pltpu.VMEM((nb, tb, HD), F32)],
        compiler_params=_cp(("parallel", "parallel", "arbitrary")),
        name="hgrn",
    )(proj3, proj3, proj3, proj3, s0, lb, g.reshape(1, HD))


def _outproj_router_kernel(att_ref, hg_ref, x_ref, wo_ref, g_ref, wr_ref, br_ref, cin_ref,
                           h_ref, xn_ref, ti_ref, gate_ref, rank_ref, cout_ref, tri_ref, cnt_ref, *, tm):
    i = pl.program_id(0)

    @pl.when(i == 0)
    def _():
        r = lax.broadcasted_iota(jnp.int32, (tm, tm), 0)
        c = lax.broadcasted_iota(jnp.int32, (tm, tm), 1)
        tri_ref[...] = jnp.where(c < r, 1.0, 0.0).astype(BF16)
        cnt_ref[...] = cin_ref[...]

    half = att_ref.shape[1]
    mix = _dot(att_ref[...], wo_ref[:half, :]) + _dot(hg_ref[...], wo_ref[half:, :])
    h = x_ref[...] + mix
    h_ref[...] = h
    xn = _rms(h, g_ref[...])
    xn_ref[...] = xn
    xn_hi = xn.astype(BF16)
    xn_lo = (xn - xn_hi.astype(F32)).astype(BF16)
    t = _dot(xn_hi, wr_ref[...])
    logits = (t[:, :N_EXPERTS] + t[:, N_EXPERTS:] + _dot(xn_lo, wr_ref[:, :N_EXPERTS])) + br_ref[...]

    lane = lax.broadcasted_iota(jnp.int32, logits.shape, 1).astype(F32)
    k_lane = lax.broadcasted_iota(jnp.int32, (tm, TOP_K), 1)
    work = logits
    sel = jnp.zeros(logits.shape, F32)
    top_v, top_i = [], []
    for _ in range(TOP_K):
        mx = jnp.max(work, axis=-1, keepdims=True)
        idx = jnp.min(jnp.where(work == mx, lane, float(N_EXPERTS)), axis=-1, keepdims=True)
        hit = lane == idx
        sel = jnp.where(hit, 1.0, sel)
        work = jnp.where(hit, -jnp.inf, work)
        top_v.append(mx)
        top_i.append(idx)
    ex = [jnp.exp(v - top_v[0]) for v in top_v]
    den = ex[0] + ex[1] + ex[2] + ex[3]
    before = _dot(tri_ref[...], sel.astype(BF16)) + cnt_ref[...]
    ti_out = jnp.zeros((tm, TOP_K), jnp.int32)
    gate_out = jnp.zeros((tm, TOP_K), F32)
    rank_out = jnp.zeros((tm, TOP_K), jnp.int32)
    for k in range(TOP_K):
        rk = jnp.sum(jnp.where(lane == top_i[k], before, 0.0), axis=-1, keepdims=True).astype(jnp.int32)
        ti_out = jnp.where(k_lane == k, top_i[k].astype(jnp.int32), ti_out)
        gate_out = jnp.where(k_lane == k, ex[k] / den, gate_out)
        rank_out = jnp.where(k_lane == k, rk, rank_out)
    ti_ref[...] = ti_out
    gate_ref[...] = gate_out
    rank_ref[...] = rank_out
    cnt_ref[...] = cnt_ref[...] + jnp.sum(sel, axis=0, keepdims=True)
    cout_ref[...] = cnt_ref[...]


def _outproj_router(att, hg, x, wo_bf16, g, wr2, br, cnt_in, tm):
    n, d = x.shape
    half = att.shape[1]
    kern = functools.partial(_outproj_router_kernel, tm=tm)
    row = lambda w: pl.BlockSpec((tm, w), lambda i: (i, 0))
    full = lambda a, b: pl.BlockSpec((a, b), lambda i: (0, 0))
    return pl.pallas_call(
        kern,
        grid=(n // tm,),
        in_specs=[row(half), row(half), row(d), full(2 * half, d), full(1, d), full(d, 2 * N_EXPERTS),
                  full(1, N_EXPERTS), full(1, N_EXPERTS)],
        out_specs=[row(d), row(d), row(TOP_K), row(TOP_K), row(TOP_K), full(1, N_EXPERTS)],
        out_shape=[jax.ShapeDtypeStruct((n, d), F32), jax.ShapeDtypeStruct((n, d), F32),
                   jax.ShapeDtypeStruct((n, TOP_K), jnp.int32), jax.ShapeDtypeStruct((n, TOP_K), F32),
                   jax.ShapeDtypeStruct((n, TOP_K), jnp.int32), jax.ShapeDtypeStruct((1, N_EXPERTS), F32)],
        scratch_shapes=[pltpu.VMEM((tm, tm), BF16), pltpu.VMEM((1, N_EXPERTS), F32)],
        compiler_params=_cp(("arbitrary",)),
        name="outproj_router",
    )(att, hg, x, wo_bf16, g.reshape(1, d), wr2, br.reshape(1, N_EXPERTS), cnt_in)


def _row_copy(src_ref, dst_ref, s, d, sem):
    return pltpu.make_async_copy(src_ref.at[pl.ds(s, 1), :], dst_ref.at[pl.ds(d, 1), :], sem)


SLOT_ALIGN = 8


def _dispatch_kernel(pos_ref, pad0_ref, padn_ref, xa_ref, xb_ref, xs_ref, sem, zero_ref, zsem, *, tm, n_a_tiles):
    i = pl.program_id(0)
    n_tail = zero_ref.shape[0]
    tail_copy = pltpu.make_async_copy(zero_ref, xs_ref.at[pl.ds(xs_ref.shape[0] - n_tail, n_tail), :], zsem)

    def pad_copy(e, j):
        return _row_copy(zero_ref, xs_ref, 0, pad0_ref[e] + j, zsem)

    def pad_rows(fn):
        def body(e, c):
            for j in range(SLOT_ALIGN - 1):
                @pl.when(j < padn_ref[e])
                def _():
                    fn(pad_copy(e, j))
            return c
        lax.fori_loop(0, N_EXPERTS, body, 0)

    @pl.when(i == 0)
    def _():
        zero_ref[...] = jnp.zeros(zero_ref.shape, zero_ref.dtype)
        tail_copy.start()
        tail_copy.wait()
        pad_rows(lambda c: c.start())
        pad_rows(lambda c: c.wait())

    def move(x_ref):
        def start(r, c):
            for k in range(TOP_K):
                _row_copy(x_ref, xs_ref, r, pos_ref[r * TOP_K + k], sem).start()
            return c

        lax.fori_loop(0, tm, start, 0)

        def wait(r, c):
            for k in range(TOP_K):
                _row_copy(x_ref, xs_ref, r, pos_ref[r * TOP_K + k], sem).wait()
            return c

        lax.fori_loop(0, tm, wait, 0)

    @pl.when(i < n_a_tiles)
    def _():
        move(xa_ref)

    @pl.when(i >= n_a_tiles)
    def _():
        move(xb_ref)


def _dispatch(pos, xa, xb, tm, n_rows, n_tail, pad0, padn):
    na, d = xa.shape
    nb = xb.shape[0]
    n_a_tiles = na // tm
    kern = functools.partial(_dispatch_kernel, tm=tm, n_a_tiles=n_a_tiles)
    return pl.pallas_call(
        kern,
        grid=((na + nb) // tm,),
        in_specs=[
            pl.BlockSpec((tm * TOP_K,), lambda i: (i,), memory_space=pltpu.SMEM),
            pl.BlockSpec(memory_space=pltpu.SMEM),
            pl.BlockSpec(memory_space=pltpu.SMEM),
            pl.BlockSpec((tm, d), lambda i: (jnp.minimum(i, n_a_tiles - 1), 0)),
            pl.BlockSpec((tm, d), lambda i: (jnp.maximum(i - n_a_tiles, 0), 0)),
        ],
        out_specs=pl.BlockSpec(memory_space=pl.ANY),
        out_shape=jax.ShapeDtypeStruct((n_rows, d), xa.dtype),
        scratch_shapes=[pltpu.SemaphoreType.DMA, pltpu.VMEM((n_tail, d), xa.dtype), pltpu.SemaphoreType.DMA],
        compiler_params=_cp(("arbitrary",)),
        name="dispatch",
    )(pos, pad0, padn, xa, xb)


EXPERT_T = 2304
EXPERT_SUB = 768
EXPERT_STG = 384
EXPERT_TF = 256


def _expert_kernel(te_ref, row0_ref, ns_ref, xs_ref, wg_ref, wu_ref, bg_ref, bu_ref, wd_ref, bd_ref, ys_ref,
                   xb_ref, acc_ref, stg_ref, wgb_ref, wub_ref, wdb_ref, sem_in, sem_out, *, n_defined):
    del te_ref
    i, f = pl.program_id(0), pl.program_id(1)
    n_f = pl.num_programs(1)
    sub, stg = EXPERT_SUB, EXPERT_STG
    n_sub = EXPERT_T // sub
    per_sub = sub // stg
    ns, r0 = ns_ref[i], row0_ref[i]

    def x_copy(p):
        src = xs_ref.at[pl.ds(pl.multiple_of(r0 + p * stg, SLOT_ALIGN), stg), :]
        return pltpu.make_async_copy(src, stg_ref.at[p % 2], sem_in.at[p % 2])

    def y_copy(item_r0, j):
        dst = ys_ref.at[pl.ds(pl.multiple_of(item_r0 + j * sub, SLOT_ALIGN), sub), :]
        return pltpu.make_async_copy(acc_ref.at[pl.ds(j * sub, sub), :], dst, sem_out)

    @pl.when((i == 0) & (f == 0))
    def _():
        n_tail = ys_ref.shape[0] - n_defined
        chunk = min(n_tail, EXPERT_T)
        acc_ref[0:chunk, :] = jnp.zeros((chunk, acc_ref.shape[1]), F32)
        for c0 in range(0, n_tail, chunk):
            c0 = min(c0, n_tail - chunk)
            tail = pltpu.make_async_copy(acc_ref.at[pl.ds(0, chunk), :], ys_ref.at[pl.ds(n_defined + c0, chunk), :],
                                         sem_out)
            tail.start()
            tail.wait()

    @pl.when(f == 0)
    def _():
        ip = jnp.maximum(i - 1, 0)
        prev_ns = jnp.where(i > 0, ns_ref[ip], 0)
        prev_r0 = row0_ref[ip]
        for j in range(n_sub):
            @pl.when(j < prev_ns)
            def _():
                y_copy(prev_r0, j).wait()

        @pl.when(ns > 0)
        def _():
            x_copy(0).start()
            x_copy(1).start()

    @pl.when(ns > 0)
    def _():
        wgb_ref[...] = wg_ref[...].astype(BF16)
        wub_ref[...] = wu_ref[...].astype(BF16)
        wdb_ref[...] = wd_ref[...].astype(BF16)

    for j in range(n_sub):
        @pl.when(j < ns)
        def _():
            rows = slice(j * sub, (j + 1) * sub)

            @pl.when(f == 0)
            def _():
                for p in range(j * per_sub, (j + 1) * per_sub):
                    x_copy(p).wait()
                    xb_ref[p * stg:(p + 1) * stg, :] = stg_ref[p % 2].astype(BF16)
                    if p + 2 < n_sub * per_sub:
                        @pl.when(p + 2 < ns * per_sub)
                        def _():
                            x_copy(p + 2).start()
                acc_ref[rows, :] = jnp.broadcast_to(bd_ref[0], (sub, acc_ref.shape[1]))

            x = xb_ref[rows, :]
            gate = jnp.minimum(_dot(x, wgb_ref[...]) + bg_ref[0], SWIGLU_LIMIT)
            up = jnp.clip(_dot(x, wub_ref[...]) + bu_ref[0], -SWIGLU_LIMIT, SWIGLU_LIMIT)
            act = (up + 1.0) * gate * jax.nn.sigmoid(SWIGLU_ALPHA * gate)
            acc_ref[rows, :] += _dot(act.astype(BF16), wdb_ref[...])

            @pl.when(f == n_f - 1)
            def _():
                y_copy(r0, j).start()

    @pl.when((i == pl.num_programs(0) - 1) & (f == n_f - 1))
    def _():
        for j in range(n_sub):
            @pl.when(j < ns)
            def _():
                y_copy(r0, j).wait()


def _experts(item_expert, item_row0, item_nsub, xs, w_gate_up, b_gate_up, w_down, b_down, n_defined):
    rows, d = xs.shape
    dff = w_down.shape[1]
    tf = EXPERT_TF
    nf = dff // tf
    n_items = item_expert.shape[0]
    grid_spec = pltpu.PrefetchScalarGridSpec(
        num_scalar_prefetch=3,
        grid=(n_items, nf),
        in_specs=[
            pl.BlockSpec(memory_space=pl.ANY),
            pl.BlockSpec((None, d, tf), lambda i, f, te, r0, ns: (te[i], 0, f)),
            pl.BlockSpec((None, d, tf), lambda i, f, te, r0, ns: (te[i], 0, nf + f)),
            pl.BlockSpec((None, 1, tf), lambda i, f, te, r0, ns: (te[i], 0, f)),
            pl.BlockSpec((None, 1, tf), lambda i, f, te, r0, ns: (te[i], 0, nf + f)),
            pl.BlockSpec((None, tf, d), lambda i, f, te, r0, ns: (te[i], f, 0)),
            pl.BlockSpec((None, 1, d), lambda i, f, te, r0, ns: (te[i], 0, 0)),
        ],
        out_specs=pl.BlockSpec(memory_space=pl.ANY),
        scratch_shapes=[pltpu.VMEM((EXPERT_T, d), BF16), pltpu.VMEM((EXPERT_T, d), F32),
                        pltpu.VMEM((2, EXPERT_STG, d), F32),
                        pltpu.VMEM((d, tf), BF16), pltpu.VMEM((d, tf), BF16), pltpu.VMEM((tf, d), BF16),
                        pltpu.SemaphoreType.DMA((2,)),
                        pltpu.SemaphoreType.DMA],
    )
    return pl.pallas_call(
        functools.partial(_expert_kernel, n_defined=n_defined),
        grid_spec=grid_spec,
        out_shape=jax.ShapeDtypeStruct((rows, d), F32),
        compiler_params=_cp(("arbitrary", "arbitrary")),
        name="experts",
    )(item_expert, item_row0, item_nsub, xs, w_gate_up, w_gate_up, b_gate_up, b_gate_up, w_down, b_down)


def _combine_kernel(pos_ref, h_ref, gate_ref, g_ref, ys_ref, o_ref, buf_ref, sem, *, tm):
    def start(r, c):
        for k in range(TOP_K):
            _row_copy(ys_ref, buf_ref.at[k], pos_ref[r * TOP_K + k], r, sem).start()
        return c

    lax.fori_loop(0, tm, start, 0)

    def wait(r, c):
        for k in range(TOP_K):
            _row_copy(ys_ref, buf_ref.at[k], pos_ref[r * TOP_K + k], r, sem).wait()
        return c

    lax.fori_loop(0, tm, wait, 0)

    gates = gate_ref[...]
    out = h_ref[...]
    for k in range(TOP_K):
        out = out + gates[:, k:k + 1] * buf_ref[k]
    o_ref[...] = _rms(out, g_ref[...])


def _combine(pos, h, gates, g, ys, tm):
    n, d = h.shape
    kern = functools.partial(_combine_kernel, tm=tm)
    return pl.pallas_call(
        kern,
        grid=(n // tm,),
        in_specs=[
            pl.BlockSpec((tm * TOP_K,), lambda i: (i,), memory_space=pltpu.SMEM),
            pl.BlockSpec((tm, d), lambda i: (i, 0)),
            pl.BlockSpec((tm, TOP_K), lambda i: (i, 0)),
            pl.BlockSpec((1, d), lambda i: (0, 0)),
            pl.BlockSpec(memory_space=pl.ANY),
        ],
        out_specs=pl.BlockSpec((tm, d), lambda i: (i, 0)),
        out_shape=jax.ShapeDtypeStruct((n, d), F32),
        scratch_shapes=[pltpu.VMEM((TOP_K, tm, d), F32), pltpu.SemaphoreType.DMA],
        compiler_params=_cp(("arbitrary",)),
        name="combine",
    )(pos, h, gates, g.reshape(1, d), ys)


TOKEN_TM = 256
SAMPLE_SEQS_PER_STEP = 8


def kernel(x_prompt, x_sample, cache_k, cache_v, state_hgrn, page_table, meta_tokens, norm_mix_g, w_in, lambda_q1, lambda_k1, lambda_q2, lambda_k2, attn_sub_g, hgrn_gamma, hgrn_norm_g, w_out, norm_ffn_g, w_router, b_router, w_gate_up, b_gate_up, w_down, b_down, norm_final_g):
    l = 0
    bp, tp, d = x_prompt.shape
    bs, ts, _ = x_sample.shape
    n_p, n_s = bp * tp, bs * ts

    lam_init = 0.8 - 0.6 * math.exp(-0.3 * l)
    lam = (jnp.exp(jnp.sum(lambda_q1[l].astype(F32) * lambda_k1[l].astype(F32)))
           - jnp.exp(jnp.sum(lambda_q2[l].astype(F32) * lambda_k2[l].astype(F32))) + lam_init).reshape(1)
    sub_scale = 1.0 - lam_init
    lb = jnp.cumsum(jax.nn.softmax(hgrn_gamma.astype(F32), axis=0), axis=0)[l].reshape(H, 1, HD)
    w_in_b = w_in[l].astype(BF16)
    w_out_b = w_out[l].astype(BF16)
    wr_hi = w_router[l].astype(BF16)
    wr_lo = (w_router[l] - wr_hi.astype(F32)).astype(BF16)
    wr2 = jnp.concatenate([wr_hi, wr_lo], axis=1)

    proj_p = _norm_matmul(x_prompt.reshape(n_p, d), norm_mix_g[l], w_in_b, 1024, SEG)
    extra = jnp.concatenate([meta_tokens.astype(F32), x_sample.reshape(n_s, d)], axis=0)
    proj_e = _norm_matmul(extra, norm_mix_g[l], w_in_b, extra.shape[0], SEG)
    proj_m, proj_s = proj_e[:N_META], proj_e[N_META:]
    proj_p3 = proj_p.reshape(bp, tp, N_SEG * SEG)

    att_p = _attn_prompt(lam, proj_p3, proj_m, attn_sub_g[l], sub_scale)
    zero_state = jnp.zeros((1, H, HD, HD), F32)
    _, s_meta = _hgrn(proj_m.reshape(1, N_META, N_SEG * SEG), zero_state, lb, hgrn_norm_g[l], N_META, N_META)
    hg_p, s_prompt = _hgrn(proj_p3, s_meta, lb, hgrn_norm_g[l], HG_CHUNK, 512)

    rows_s = lambda k: proj_s[:, k * SEG:(k + 1) * SEG].reshape(bs, ts * H, HD)
    att_s = _attn_decode(lam, page_table, rows_s(0), rows_s(1), rows_s(2), cache_k[l], cache_v[l],
                         attn_sub_g[l], sub_scale)
    hg_s, s_sample = _hgrn(proj_s.reshape(bs, ts, N_SEG * SEG), state_hgrn[l].astype(F32), lb, hgrn_norm_g[l], ts, ts,
                           nb=SAMPLE_SEQS_PER_STEP)

    cnt0 = jnp.zeros((1, N_EXPERTS), F32)
    h_p, xn_p, ti_p, gate_p, rank_p, cnt1 = _outproj_router(
        att_p.reshape(n_p, SEG), hg_p.reshape(n_p, SEG), x_prompt.reshape(n_p, d), w_out_b, norm_ffn_g[l],
        wr2, b_router[l], cnt0, TOKEN_TM)
    h_s, xn_s, ti_s, gate_s, rank_s, cnt2 = _outproj_router(
        att_s.reshape(n_s, SEG), hg_s.reshape(n_s, SEG), x_sample.reshape(n_s, d), w_out_b, norm_ffn_g[l],
        wr2, b_router[l], cnt1, TOKEN_TM)

    n_slots = (n_p + n_s) * TOP_K + N_EXPERTS * (SLOT_ALIGN - 1)
    n_items = n_slots // EXPERT_T + N_EXPERTS
    counts = cnt2[0].astype(jnp.int32)
    counts_al = (counts + SLOT_ALIGN - 1) // SLOT_ALIGN * SLOT_ALIGN
    offsets = jnp.cumsum(counts_al) - counts_al
    items_per = (counts + EXPERT_T - 1) // EXPERT_T
    item_end = jnp.cumsum(items_per)
    item_start = item_end - items_per
    item_ids = jnp.arange(n_items, dtype=jnp.int32)
    item_valid = item_ids < item_end[-1]
    item_expert = jnp.minimum(jnp.sum((item_end[None, :] <= item_ids[:, None]).astype(jnp.int32), axis=1),
                              N_EXPERTS - 1)
    item_k = item_ids - item_start[item_expert]
    item_rows = jnp.where(item_valid, jnp.clip(counts[item_expert] - item_k * EXPERT_T, 0, EXPERT_T), 0)
    item_row0 = jnp.where(item_valid, offsets[item_expert] + item_k * EXPERT_T, 0)
    item_nsub = (item_rows + EXPERT_SUB - 1) // EXPERT_SUB
    last_expert = item_expert[jnp.maximum(item_end[-1] - 1, 0)]
    item_expert = jnp.where(item_valid, item_expert, last_expert)
    pos_p = (offsets[ti_p] + rank_p).reshape(-1)
    pos_s = (offsets[ti_s] + rank_s).reshape(-1)

    n_routed = (n_p + n_s) * TOP_K
    n_rows = n_slots + EXPERT_SUB
    xs = _dispatch(jnp.concatenate([pos_p, pos_s]), xn_p, xn_s, TOKEN_TM, n_rows, n_rows - n_routed,
                   offsets + counts, counts_al - counts)
    ys = _experts(item_expert, item_row0, item_nsub, xs, w_gate_up[l],
                  b_gate_up[l].reshape(N_EXPERTS, 1, 2 * D_FF), w_down[l], b_down[l].reshape(N_EXPERTS, 1, d),
                  n_routed)
    y_p = _combine(pos_p, h_p, gate_p, norm_final_g, ys, TOKEN_TM)
    y_s = _combine(pos_s, h_s, gate_s, norm_final_g, ys, TOKEN_TM)

    def kv_prompt(k):
        own = proj_p3[:, :, k * SEG:(k + 1) * SEG]
        meta = jnp.broadcast_to(proj_m[None, :, k * SEG:(k + 1) * SEG], (bp, N_META, SEG))
        return jnp.concatenate([meta, own], axis=1).reshape(1, bp, N_META + tp, H, HD)

    k_sample = proj_s[:, SEG:2 * SEG].reshape(1, bs, ts, H, HD)
    v_sample = proj_s[:, 2 * SEG:3 * SEG].reshape(1, bs, ts, H, HD)
    return (y_p.reshape(bp, tp, d), y_s.reshape(bs, ts, d), kv_prompt(1), kv_prompt(2), s_prompt[None],
            k_sample, v_sample, s_sample[None].astype(state_hgrn.dtype))
```
